```python
import math, functools
import jax, jax.numpy as jnp
from jax import lax
import numpy as np

D_MODEL = 2048
BATCH = 16
SEQ = 256
DEPTH = 4
DEC_BATCH = 8
DEC_SEQ = 4096
PAST_LEN = 512

GRID_W = 64
GLA_HEADS = 4
GLA_QK = D_MODEL // 4
GLA_V = D_MODEL // 2
GLA_DK = GLA_QK // GLA_HEADS
GLA_DV = GLA_V // GLA_HEADS
GLA_LR = 16
GLA_TAU = 16.0
GLA_CHUNK = 64
FN_GROUPS = 4
FN_W = D_MODEL // 2
FN_GROUP_W = FN_W // FN_GROUPS
IN_SPLITS = (GLA_QK, 2 * GLA_QK, 2 * GLA_QK + GLA_V, 2 * GLA_QK + 2 * GLA_V,
             2 * GLA_QK + 2 * GLA_V + 2 * GLA_LR,
             2 * GLA_QK + 2 * GLA_V + 2 * GLA_LR + FN_W,
             2 * GLA_QK + 2 * GLA_V + 2 * GLA_LR + FN_W + D_MODEL)
N_IN = IN_SPLITS[-1] + D_MODEL
D_FF = ((8 * D_MODEL // 3 + 255) // 256) * 256
N_EXPERTS = 8
TOP_K = 2
D_EXP = 7 * D_MODEL // 2
N_DENSE = (DEPTH + 1) // 2
N_MOE = DEPTH // 2
EPS = 1e-6

kernel_name = 'hybrid_gla_fnet_moe_diffusion_step'


def rmsnorm(x, g):
    xf = x.astype(jnp.float32)
    y = xf * lax.rsqrt(jnp.mean(xf * xf, axis=-1, keepdims=True) + EPS)
    return (y * g.astype(jnp.float32)).astype(x.dtype)


def gla_chunked(q, k, v, log_a, s0):
    bsz, n, h, _ = q.shape
    dv = v.shape[-1]
    nc = n // GLA_CHUNK

    def chunks(t):
        return t.astype(jnp.float32).reshape(bsz, nc, GLA_CHUNK, h, t.shape[-1])

    q, k, v, log_a = chunks(q), chunks(k), chunks(v), chunks(log_a)
    cum = jnp.cumsum(log_a, axis=2)
    last = cum[:, :, -1:]
    q_dec = q * jnp.exp(cum)
    k_inv = k * jnp.exp(-cum)
    k_end = k * jnp.exp(last - cum)
    causal = jnp.tril(jnp.ones((GLA_CHUNK, GLA_CHUNK), dtype=bool))
    scores = jnp.einsum('bnihk,bnjhk->bnhij', q_dec, k_inv)
    scores = jnp.where(causal, scores, 0.0)
    o_intra = jnp.einsum('bnhij,bnjhv->bnihv', scores, v)
    kv = jnp.einsum('bnjhk,bnjhv->bnhkv', k_end, v)
    decay = jnp.exp(last[:, :, 0])

    def step(s, inp):
        a_c, kv_c = inp
        return a_c[..., None] * s + kv_c, s

    s_fin, s_in = lax.scan(step, s0.astype(jnp.float32),
                           (jnp.moveaxis(decay, 1, 0), jnp.moveaxis(kv, 1, 0)))
    o_inter = jnp.einsum('bnihk,nbhkv->bnihv', q_dec, s_in)
    o = (o_intra + o_inter).reshape(bsz, n, h, dv)
    return o, s_fin


def token_mixer(h, s_f0, s_b0, w_in, w_alpha_up, b_alpha, g_head, w_gla_br, w_fn_br, w_out):
    bsz, n, _ = h.shape
    u = h @ w_in
    q, k, v, r, a_lr, f, gate_a, gate_b = jnp.split(u, IN_SPLITS, axis=-1)
    q = q.reshape(bsz, n, GLA_HEADS, GLA_DK) * (GLA_DK ** -0.5)
    k = k.reshape(bsz, n, GLA_HEADS, GLA_DK)
    v = v.reshape(bsz, n, GLA_HEADS, GLA_DV)
    z = jnp.einsum('bndr,drk->bndk', a_lr.reshape(bsz, n, 2, GLA_LR), w_alpha_up) + b_alpha
    log_a = (jax.nn.log_sigmoid(z.astype(jnp.float32)) / GLA_TAU).reshape(bsz, n, 2, GLA_HEADS, GLA_DK)
    o_f, s_f = gla_chunked(q, k, v, log_a[:, :, 0], s_f0)
    flip = lambda t: jnp.flip(t, axis=1)
    o_b, s_b = gla_chunked(flip(q), flip(k), flip(v), flip(log_a[:, :, 1]), s_b0)
    o = o_f + flip(o_b)
    o = o * lax.rsqrt(jnp.mean(o * o, axis=-1, keepdims=True) + EPS) * g_head.astype(jnp.float32).reshape(GLA_HEADS, GLA_DV)
    o = (o.reshape(bsz, n, GLA_V) * jax.nn.silu(r.astype(jnp.float32))).astype(h.dtype)
    branch_a = o @ w_gla_br
    fr = jnp.fft.fft2(f.astype(jnp.float32).reshape(bsz, n, FN_GROUPS, FN_GROUP_W),
                      axes=(1, 3), norm='ortho').real
    branch_b = fr.reshape(bsz, n, FN_W).astype(h.dtype) @ w_fn_br
    merged = jax.nn.sigmoid(gate_a) * branch_a + jax.nn.sigmoid(gate_b) * branch_b
    return merged @ w_out, s_f, s_b


def swiglu(x, w_gate, w_up, w_down):
    return (jax.nn.silu(x @ w_gate) * (x @ w_up)) @ w_down


def moe_swiglu(x, w_router, w_gate, w_up, w_down):
    logits = (x @ w_router).astype(jnp.float32)
    top_logit, top_idx = lax.top_k(logits, TOP_K)
    top_w = jax.nn.softmax(top_logit, axis=-1)
    combine = jnp.einsum('bnk,bnke->bne', top_w,
                         jax.nn.one_hot(top_idx, N_EXPERTS, dtype=jnp.float32)).astype(x.dtype)
    y = jnp.zeros_like(x)
    for e in range(N_EXPERTS):
        y = y + combine[..., e:e + 1] * swiglu(x, w_gate[e], w_up[e], w_down[e])
    return y


def trunk_layer(x, cond, s_f0, s_b0, w_ada, b_ada, g_mix_pre, g_mix_post, w_in, w_alpha_up,
                b_alpha, g_head, w_gla_br, w_fn_br, w_out, g_ffn_pre, g_ffn_post, ffn):
    mod = (jax.nn.silu(cond) @ w_ada + b_ada)[:, None, :]
    sh1, sc1, ga1, sh2, sc2, ga2 = jnp.split(mod, 6, axis=-1)
    h = rmsnorm(x, g_mix_pre) * (1.0 + sc1) + sh1
    o, s_f, s_b = token_mixer(h, s_f0, s_b0, w_in, w_alpha_up, b_alpha, g_head, w_gla_br, w_fn_br, w_out)
    x = x + ga1 * rmsnorm(o, g_mix_post)
    h = rmsnorm(x, g_ffn_pre) * (1.0 + sc2) + sh2
    x = x + ga2 * rmsnorm(ffn(h), g_ffn_post)
    return x, s_f, s_b


def setup_inputs(seed: int = 0) -> dict:
    key = jax.random.key(seed)
    ks = iter(jax.random.split(key, 32))
    d = D_MODEL

    def nrm(shape, scale):
        return jax.random.normal(next(ks), shape, jnp.float32) * scale

    def gain(shape):
        return 1.0 + nrm(shape, 0.02)

    return {
        'x_prompt': nrm((BATCH, SEQ, d), 1.0),
        'x_sample': nrm((DEC_BATCH, DEC_SEQ, d), 1.0),
        'state_gla': nrm((DEC_BATCH, DEPTH, 2, GLA_HEADS, GLA_DK, GLA_DV), 1.0),
        'c': nrm((DEC_BATCH, d), 1.0),
        'c_ctx': nrm((d,), 1.0),
        'w_ada': nrm((DEPTH, d, 6 * d), 0.5 * d ** -0.5),
        'b_ada': nrm((DEPTH, 6 * d), 0.01),
        'g_mix_pre': gain((DEPTH, d)),
        'g_mix_post': gain((DEPTH, d)),
        'w_in': nrm((DEPTH, d, N_IN), d ** -0.5),
        'w_alpha_up': nrm((DEPTH, 2, GLA_LR, GLA_QK), GLA_LR ** -0.5),
        'b_alpha': nrm((DEPTH, 2, GLA_QK), 0.1),
        'g_head': gain((DEPTH, GLA_V)),
        'w_gla_br': nrm((DEPTH, GLA_V, d), GLA_V ** -0.5),
        'w_fn_br': nrm((DEPTH, FN_W, d), FN_W ** -0.5),
        'w_out': nrm((DEPTH, d, d), d ** -0.5),
        'g_ffn_pre': gain((DEPTH, d)),
        'g_ffn_post': gain((DEPTH, d)),
        'w_ffn_gate': nrm((N_DENSE, d, D_FF), d ** -0.5),
        'w_ffn_up': nrm((N_DENSE, d, D_FF), d ** -0.5),
        'w_ffn_down': nrm((N_DENSE, D_FF, d), D_FF ** -0.5),
        'w_router': nrm((N_MOE, d, N_EXPERTS), d ** -0.5),
        'w_exp_gate': nrm((N_MOE, N_EXPERTS, d, D_EXP), d ** -0.5),
        'w_exp_up': nrm((N_MOE, N_EXPERTS, d, D_EXP), d ** -0.5),
        'w_exp_down': nrm((N_MOE, N_EXPERTS, D_EXP, d), D_EXP ** -0.5),
    }


def reference(x_prompt, x_sample, state_gla, c, c_ctx, w_ada, b_ada, g_mix_pre, g_mix_post,
              w_in, w_alpha_up, b_alpha, g_head, w_gla_br, w_fn_br, w_out, g_ffn_pre,
              g_ffn_post, w_ffn_gate, w_ffn_up, w_ffn_down, w_router, w_exp_gate,
              w_exp_up, w_exp_down):
    n_lat = x_sample.shape[1]
    rows = n_lat // GRID_W
    assert rows * GRID_W == n_lat
    zero_state = jnp.zeros((x_prompt.shape[0], GLA_HEADS, GLA_DK, GLA_DV), jnp.float32)
    cond_ctx = c_ctx[None, :]
    xp, xs = x_prompt, x_sample
    ctx_states = []
    for l in range(DEPTH):
        j = l // 2
        if l % 2 == 0:
            ffn = functools.partial(swiglu, w_gate=w_ffn_gate[j], w_up=w_ffn_up[j], w_down=w_ffn_down[j])
        else:
            ffn = functools.partial(moe_swiglu, w_router=w_router[j], w_gate=w_exp_gate[j],
                                    w_up=w_exp_up[j], w_down=w_exp_down[j])
        lw = (w_ada[l], b_ada[l], g_mix_pre[l], g_mix_post[l], w_in[l], w_alpha_up[l], b_alpha[l],
              g_head[l], w_gla_br[l], w_fn_br[l], w_out[l], g_ffn_pre[l], g_ffn_post[l])
        xp, s_f, s_b = trunk_layer(xp, cond_ctx, zero_state, zero_state, *lw, ffn=ffn)
        ctx_states.append(jnp.stack([s_f, s_b], axis=1))
        xs, _, _ = trunk_layer(xs, c, state_gla[:, l, 0], state_gla[:, l, 1], *lw, ffn=ffn)
    new_state_gla = jnp.stack(ctx_states, axis=1).astype(x_prompt.dtype)
    return (xp, xs, new_state_gla)
```

```python
import functools
import math

import jax
import jax.numpy as jnp
from jax import lax
from jax.experimental import pallas as pl
from jax.experimental.pallas import tpu as pltpu

GLA_HEADS = 4
GLA_LR = 16
GLA_TAU = 16.0
GLA_CHUNK = 64
FN_GROUPS = 4
TOP_K = 2
EPS = 1e-6

LANES = 128
SLAB = 4 * GLA_CHUNK
VMEM_LIMIT = 56 * 1024 * 1024
BF16 = jnp.bfloat16
F32 = jnp.float32


def _pick(n, pref, mult=8):
    t = min(n, pref)
    t -= t % mult
    while t > mult and n % t:
        t -= mult
    assert t > 0 and n % t == 0, (n, pref)
    return t


def _params(*sem):
    return pltpu.CompilerParams(dimension_semantics=sem, vmem_limit_bytes=VMEM_LIMIT)


def _silu(x):
    return x / (1.0 + jnp.exp(-x))


def _sigmoid(x):
    return 1.0 / (1.0 + jnp.exp(-x))


def _dot(a, b):
    return jnp.dot(a, b, preferred_element_type=F32)


def _dot_nt(a, b):
    return lax.dot_general(a, b, (((1,), (1,)), ((), ())), preferred_element_type=F32)


def _dot_tn(a, b):
    return lax.dot_general(a, b, (((0,), (0,)), ((), ())), preferred_element_type=F32)


def _pack_pair(h):
    half = h.shape[1] // 2
    bits = lax.bitcast_convert_type(h.astype(BF16).astype(F32), jnp.uint32)
    return (bits[:, half:] & jnp.uint32(0xFFFF0000)) | (bits[:, :half] >> 16)


def _unpack_pair(u):
    lo = lax.bitcast_convert_type(u << 16, F32)
    hi = lax.bitcast_convert_type(u & jnp.uint32(0xFFFF0000), F32)
    return lo, hi


def _ada_kernel(c_ref, w_ref, b_ref, o_ref):
    s = _silu(c_ref[...]).astype(BF16)
    o_ref[...] = _dot(s, w_ref[...].astype(BF16)) + b_ref[...]


def _ada(cond, w_ada, b_ada):
    depth, d, n6 = w_ada.shape
    rows = cond.shape[0]
    tn = _pick(n6, 1536, LANES)
    return pl.pallas_call(
        _ada_kernel,
        grid=(depth, n6 // tn),
        in_specs=[pl.BlockSpec((rows, d), lambda l, j: (0, 0)),
                  pl.BlockSpec((None, d, tn), lambda l, j: (l, 0, j)),
                  pl.BlockSpec((None, 1, tn), lambda l, j: (l, 0, j))],
        out_specs=pl.BlockSpec((None, rows, tn), lambda l, j: (l, 0, j)),
        out_shape=jax.ShapeDtypeStruct((depth, rows, n6), F32),
        compiler_params=_params("parallel", "parallel"),
        name="ada",
    )(cond, w_ada, b_ada.reshape(depth, 1, n6))


def _modulated_norm(x_ref, mod_ref, g_ref, sh_i, sc_i):
    x = x_ref[...]
    y = x * lax.rsqrt(jnp.mean(x * x, axis=-1, keepdims=True) + EPS) * g_ref[...]
    return y * (1.0 + mod_ref[sc_i:sc_i + 1, :]) + mod_ref[sh_i:sh_i + 1, :]


def _prenorm_kernel(x_ref, mod_ref, g_ref, o_ref, *, sh_i, sc_i, pack):
    h = _modulated_norm(x_ref, mod_ref, g_ref, sh_i, sc_i)
    o_ref[...] = _pack_pair(h) if pack else h.astype(BF16)


def _prenorm_route_kernel(x_ref, mod_ref, g_ref, wr_ref, o_ref, info_ref, cnt_ref, carry, *,
                          sh_i, sc_i, n_exp):
    i = pl.program_id(0)

    @pl.when(i == 0)
    def _():
        carry[...] = jnp.zeros_like(carry)

    h = _modulated_norm(x_ref, mod_ref, g_ref, sh_i, sc_i)
    o_ref[...] = _pack_pair(h)
    w = wr_ref[...]
    hh = h.astype(BF16)
    hl = (h - hh.astype(F32)).astype(BF16)
    wh = w.astype(BF16)
    wl = (w - wh.astype(F32)).astype(BF16)
    logits = _dot(hh, wh) + _dot(hl, wh) + _dot(hh, wl)
    tm = logits.shape[0]
    lane = lax.broadcasted_iota(jnp.int32, (tm, LANES), 1).astype(F32)
    neg = jnp.float32(-jnp.inf)
    l1 = jnp.where(lane < n_exp, logits, neg)
    m1 = jnp.max(l1, axis=-1, keepdims=True)
    i1 = jnp.min(jnp.where(l1 == m1, lane, float(LANES)), axis=-1, keepdims=True)
    l2 = jnp.where(lane == i1, neg, l1)
    m2 = jnp.max(l2, axis=-1, keepdims=True)
    i2 = jnp.min(jnp.where(l2 == m2, lane, float(LANES)), axis=-1, keepdims=True)
    e = jnp.exp(m2 - m1)
    w1 = 1.0 / (1.0 + e)
    w2 = e / (1.0 + e)
    sel1 = lane == i1
    sel2 = lane == i2
    a = jnp.where(sel1 | sel2, 1.0, 0.0)
    r = lax.broadcasted_iota(jnp.int32, (tm, tm), 0)
    c = lax.broadcasted_iota(jnp.int32, (tm, tm), 1)
    before = jnp.where(c < r, 1.0, 0.0).astype(BF16)
    pos = _dot(before, a.astype(BF16)) + carry[...]
    p1 = jnp.sum(jnp.where(sel1, pos, 0.0), axis=-1, keepdims=True)
    p2 = jnp.sum(jnp.where(sel2, pos, 0.0), axis=-1, keepdims=True)
    total = carry[...] + jnp.sum(a, axis=0, keepdims=True)
    carry[...] = total
    cnt_ref[...] = jnp.broadcast_to(total, cnt_ref.shape)
    info = jnp.where(lane == 0, i1, 0.0)
    info = jnp.where(lane == 1, i2, info)
    info = jnp.where(lane == 2, p1, info)
    info = jnp.where(lane == 3, p2, info)
    info = jnp.where(lane == 4, w1, info)
    info = jnp.where(lane == 5, w2, info)
    info_ref[...] = info


def _mod_spec(d, tm, row_of):
    return pl.BlockSpec((None, 6, d), lambda i: (row_of(i * tm), 0, 0))


def _prenorm(x, mod, g, row_of, sh_i, sc_i, pack):
    t, d = x.shape
    tm = _pick(t, 512)
    out = jax.ShapeDtypeStruct((t, d // 2), jnp.uint32) if pack else jax.ShapeDtypeStruct((t, d), BF16)
    ow = d // 2 if pack else d
    return pl.pallas_call(
        functools.partial(_prenorm_kernel, sh_i=sh_i, sc_i=sc_i, pack=pack),
        grid=(t // tm,),
        in_specs=[pl.BlockSpec((tm, d), lambda i: (i, 0)),
                  _mod_spec(d, tm, row_of),
                  pl.BlockSpec((1, d), lambda i: (0, 0))],
        out_specs=pl.BlockSpec((tm, ow), lambda i: (i, 0)),
        out_shape=out,
        compiler_params=_params("parallel"),
        name="prenorm",
    )(x, mod, g.reshape(1, d))


def _prenorm_route(x, mod, g, w_router, row_of, sh_i, sc_i):
    t, d = x.shape
    n_exp = w_router.shape[1]
    tm = _pick(t, 512)
    wr = jnp.zeros((d, LANES), F32).at[:, :n_exp].set(w_router)
    return pl.pallas_call(
        functools.partial(_prenorm_route_kernel, sh_i=sh_i, sc_i=sc_i, n_exp=n_exp),
        grid=(t // tm,),
        in_specs=[pl.BlockSpec((tm, d), lambda i: (i, 0)),
                  _mod_spec(d, tm, row_of),
                  pl.BlockSpec((1, d), lambda i: (0, 0)),
                  pl.BlockSpec((d, LANES), lambda i: (0, 0))],
        out_specs=[pl.BlockSpec((tm, d // 2), lambda i: (i, 0)),
                   pl.BlockSpec((tm, LANES), lambda i: (i, 0)),
                   pl.BlockSpec((8, LANES), lambda i: (0, 0))],
        out_shape=[jax.ShapeDtypeStruct((t, d // 2), jnp.uint32),
                   jax.ShapeDtypeStruct((t, LANES), F32),
                   jax.ShapeDtypeStruct((8, LANES), F32)],
        scratch_shapes=[pltpu.VMEM((1, LANES), F32)],
        compiler_params=_params("arbitrary"),
        name="prenorm_route",
    )(x, mod, g.reshape(1, d), wr)


def _matmul_kernel(x_ref, w_ref, o_ref):
    o_ref[...] = _dot(x_ref[...], w_ref[...]).astype(o_ref.dtype)


def _matmul(x, w, tm_pref=1024, tn_pref=1664):
    t, k = x.shape
    n = w.shape[1]
    tm = _pick(t, tm_pref)
    tn = _pick(n, tn_pref, LANES)
    return pl.pallas_call(
        _matmul_kernel,
        grid=(t // tm, n // tn),
        in_specs=[pl.BlockSpec((tm, k), lambda i, j: (i, 0)),
                  pl.BlockSpec((k, tn), lambda i, j: (0, j))],
        out_specs=pl.BlockSpec((tm, tn), lambda i, j: (i, j)),
        out_shape=jax.ShapeDtypeStruct((t, n), BF16),
        compiler_params=_params("parallel", "parallel"),
        name="in_proj",
    )(x, w)


def _gla_kernel(*refs, n_rows, has_s0, want_state):
    q_ref, k_ref, v_ref, r_ref, alr_ref, wal_ref, bal_ref, gh_ref = refs[:8]
    pos = 8
    s0_ref = None
    if has_s0:
        s0_ref = refs[pos]
        pos += 1
    og_ref = refs[pos]
    pos += 1
    sfin_ref = None
    if want_state:
        sfin_ref = refs[pos]
        pos += 1
    of_scr, ob_scr, st_scr = refs[pos:pos + 3]
    o_scr = (of_scr, ob_scr)

    dk = q_ref.shape[1]
    n_slab = n_rows // SLAB
    n_chunk = SLAB // GLA_CHUNK
    scale = dk ** -0.5

    ri = lax.broadcasted_iota(jnp.int32, (SLAB, SLAB), 0)
    ci = lax.broadcasted_iota(jnp.int32, (SLAB, SLAB), 1)
    shift = GLA_CHUNK.bit_length() - 1
    same = jnp.right_shift(ri, shift) == jnp.right_shift(ci, shift)
    keep = (same & (ci <= ri), same & (ci >= ri))
    ones = jnp.where(same, 1.0, 0.0).astype(BF16)
    sum_mats = tuple(jnp.concatenate([jnp.where(m, 1.0, 0.0).astype(BF16), ones], axis=0) for m in keep)

    for d in range(2):
        if has_s0:
            st_scr[d] = s0_ref[d].T
        else:
            st_scr[d] = jnp.zeros(st_scr.shape[1:], F32)

    def slab(d, r0):
        rows = pl.ds(r0, SLAB)
        cols = slice(d * dk, (d + 1) * dk)
        z = _dot(alr_ref[rows, :], wal_ref[:, cols]) + bal_ref[:, cols]
        la = (jnp.minimum(z, 0.0) - jnp.log1p(jnp.exp(-jnp.abs(z)))) * (1.0 / GLA_TAU)
        hi = la.astype(BF16)
        rem = la - hi.astype(F32)
        mid = rem.astype(BF16)
        lo = (rem - mid.astype(F32)).astype(BF16)
        ct = _dot(sum_mats[d], jnp.concatenate([hi, mid, lo], axis=1))
        cum = ct[:SLAB, :dk] + ct[:SLAB, dk:2 * dk] + ct[:SLAB, 2 * dk:]
        tot = ct[SLAB:, :dk] + ct[SLAB:, dk:2 * dk] + ct[SLAB:, 2 * dk:]
        q = q_ref[rows, :].astype(F32) * scale
        k = k_ref[rows, :].astype(F32)
        qd = (q * jnp.exp(cum)).astype(BF16)
        ki = (k * jnp.exp(-cum)).astype(BF16)
        ke = (k * jnp.exp(tot - cum)).astype(BF16)
        dec = jnp.exp(tot)
        v = v_ref[rows, :]
        s = jnp.where(keep[d], _dot_nt(qd, ki), 0.0).astype(BF16)
        o = _dot(s, v)
        st = st_scr[d]
        order = range(n_chunk) if d == 0 else range(n_chunk - 1, -1, -1)
        for c in order:
            sl = slice(c * GLA_CHUNK, (c + 1) * GLA_CHUNK)
            o_inter = _dot_nt(qd[sl], st.astype(BF16))
            o_scr[d][pl.ds(r0 + c * GLA_CHUNK, GLA_CHUNK), :] = o[sl] + o_inter
            st = dec[c * GLA_CHUNK:c * GLA_CHUNK + 1, :] * st + _dot_tn(v[sl], ke[sl])
        st_scr[d] = st

    def body(i, carry):
        slab(0, pl.multiple_of(i * SLAB, SLAB))
        slab(1, pl.multiple_of((n_slab - 1 - i) * SLAB, SLAB))
        return carry

    lax.fori_loop(0, n_slab, body, 0)

    def finish(i, carry):
        rows = pl.ds(pl.multiple_of(i * SLAB, SLAB), SLAB)
        o = of_scr[rows, :] + ob_scr[rows, :]
        o = o * lax.rsqrt(jnp.mean(o * o, axis=-1, keepdims=True) + EPS) * gh_ref[...]
        og_ref[rows, :] = (o * _silu(r_ref[rows, :].astype(F32))).astype(og_ref.dtype)
        return carry

    lax.fori_loop(0, n_slab, finish, 0)
    if want_state:
        for d in range(2):
            sfin_ref[d] = st_scr[d].T


def _gla(u, wal, bal, gh, s0, layer, *, row0, n_seq, n_rows, col, want_state):
    t = u.shape[0]
    heads, dk2 = wal.shape[0], wal.shape[2]
    dk = dk2 // 2
    dv = gh.shape[2]
    assert row0 % n_rows == 0 and n_rows % SLAB == 0
    rb0 = row0 // n_rows
    has_s0 = s0 is not None

    def at(col_units):
        return lambda b, h: (rb0 + b, col_units + h)

    in_specs = [pl.BlockSpec((n_rows, dk), at(col["q"] // dk)),
                pl.BlockSpec((n_rows, dk), at(col["k"] // dk)),
                pl.BlockSpec((n_rows, dv), at(col["v"] // dv)),
                pl.BlockSpec((n_rows, dv), at(col["r"] // dv)),
                pl.BlockSpec((n_rows, LANES), lambda b, h: (rb0 + b, col["alr"] // LANES)),
                pl.BlockSpec((None, LANES, dk2), lambda b, h: (h, 0, 0)),
                pl.BlockSpec((None, 1, dk2), lambda b, h: (h, 0, 0)),
                pl.BlockSpec((None, 1, dv), lambda b, h: (h, 0, 0))]
    args = [u, u, u, u, u, wal, bal, gh]
    if has_s0:
        in_specs.append(pl.BlockSpec((None, None, 2, None, dk, dv), lambda b, h: (b, layer, 0, h, 0, 0)))
        args.append(s0)
    out_specs = [pl.BlockSpec((n_rows, dv), lambda b, h: (b, h))]
    out_shape = [jax.ShapeDtypeStruct((n_seq * n_rows, heads * dv), BF16)]
    if want_state:
        out_specs.append(pl.BlockSpec((None, 2, None, dk, dv), lambda b, h: (b, 0, h, 0, 0)))
        out_shape.append(jax.ShapeDtypeStruct((n_seq, 2, heads, dk, dv), F32))
    res = pl.pallas_call(
        functools.partial(_gla_kernel, n_rows=n_rows, has_s0=has_s0, want_state=want_state),
        grid=(n_seq, heads),
        in_specs=in_specs,
        out_specs=out_specs,
        out_shape=out_shape,
        scratch_shapes=[pltpu.VMEM((n_rows, dv), F32), pltpu.VMEM((n_rows, dv), F32),
                        pltpu.VMEM((2, dv, dk), F32)],
        compiler_params=_params("parallel", "parallel"),
        name="gla",
    )(*args)
    return res if want_state else (res[0], None)


def _chan_dft_kernel(x_ref, tab_ref, xc_ref, xs_ref):
    y = _dot(x_ref[...], tab_ref[...])
    w = xc_ref.shape[1]
    xc_ref[...] = y[:, :w].astype(xc_ref.dtype)
    xs_ref[...] = y[:, w:].astype(xs_ref.dtype)


def _chan_dft(u, tab, col_f, width):
    t = u.shape[0]
    gw = tab.shape[0]
    tm = _pick(t, 2048)
    c0 = col_f // gw
    spec_o = pl.BlockSpec((tm, gw), lambda i, g: (i, g))
    return pl.pallas_call(
        _chan_dft_kernel,
        grid=(t // tm, width // gw),
        in_specs=[pl.BlockSpec((tm, gw), lambda i, g: (i, c0 + g)),
                  pl.BlockSpec((gw, 2 * gw), lambda i, g: (0, 0))],
        out_specs=[spec_o, spec_o],
        out_shape=[jax.ShapeDtypeStruct((t, width), BF16)] * 2,
        compiler_params=_params("parallel", "parallel"),
        name="chan_dft",
    )(u, tab)


def _pos_dft_kernel(tc_ref, ts_ref, xc_ref, xs_ref, o_ref):
    o_ref[...] = (_dot(tc_ref[...], xc_ref[...]) + _dot(ts_ref[...], xs_ref[...])).astype(o_ref.dtype)


def _pos_dft(tc, ts, xc, xs, *, row0, n_seq, n_rows):
    width = xc.shape[1]
    assert row0 % n_rows == 0
    rb0 = row0 // n_rows
    to = _pick(n_rows, 512)
    nt = n_rows // to
    once = pl.Buffered(1)
    return pl.pallas_call(
        _pos_dft_kernel,
        grid=(n_seq, nt),
        in_specs=[pl.BlockSpec((to, n_rows), lambda b, i: (i, 0)),
                  pl.BlockSpec((to, n_rows), lambda b, i: (i, 0)),
                  pl.BlockSpec((n_rows, width), lambda b, i: (rb0 + b, 0), pipeline_mode=once),
                  pl.BlockSpec((n_rows, width), lambda b, i: (rb0 + b, 0), pipeline_mode=once)],
        out_specs=pl.BlockSpec((to, width), lambda b, i: (b * nt + i, 0)),
        out_shape=jax.ShapeDtypeStruct((n_seq * n_rows, width), BF16),
        compiler_params=_params("parallel", "parallel"),
        name="pos_dft",
    )(tc, ts, xc, xs)


def _dft_tables(n, scale):
    j = jnp.arange(n, dtype=jnp.int32)
    ang = ((j[:, None] * j[None, :]) % n).astype(F32) * (2.0 * math.pi / n)
    return jnp.cos(ang) * scale, jnp.sin(ang) * scale


def _mix_out_kernel(og_ref, fr_ref, ga_ref, gb_ref, x_ref, mod_ref, g_ref, wa_ref, wb_ref, wo_ref,
                    o_ref, *, ga_i):
    ba = _dot(og_ref[...], wa_ref[...])
    bb = _dot(fr_ref[...], wb_ref[...])
    merged = _sigmoid(ga_ref[...].astype(F32)) * ba + _sigmoid(gb_ref[...].astype(F32)) * bb
    out = _dot(merged.astype(BF16), wo_ref[...])
    y = out * lax.rsqrt(jnp.mean(out * out, axis=-1, keepdims=True) + EPS) * g_ref[...]
    o_ref[...] = x_ref[...] + mod_ref[ga_i:ga_i + 1, :] * y


def _mix_out(og, fr, u, x, mod, g, wa, wb, wo, row_of, col, ga_i):
    t, d = x.shape
    w = og.shape[1]
    tm = _pick(t, 256)
    once = pl.Buffered(1)
    return pl.pallas_call(
        functools.partial(_mix_out_kernel, ga_i=ga_i),
        grid=(t // tm,),
        in_specs=[pl.BlockSpec((tm, w), lambda i: (i, 0)),
                  pl.BlockSpec((tm, w), lambda i: (i, 0)),
                  pl.BlockSpec((tm, d), lambda i: (i, col["ga"] // d)),
                  pl.BlockSpec((tm, d), lambda i: (i, col["gb"] // d)),
                  pl.BlockSpec((tm, d), lambda i: (i, 0)),
                  _mod_spec(d, tm, row_of),
                  pl.BlockSpec((1, d), lambda i: (0, 0)),
                  pl.BlockSpec((w, d), lambda i: (0, 0), pipeline_mode=once),
                  pl.BlockSpec((w, d), lambda i: (0, 0), pipeline_mode=once),
                  pl.BlockSpec((d, d), lambda i: (0, 0), pipeline_mode=once)],
        out_specs=pl.BlockSpec((tm, d), lambda i: (i, 0)),
        out_shape=jax.ShapeDtypeStruct((t, d), F32),
        compiler_params=_params("parallel"),
        name="mix_out",
    )(og, fr, u, u, x, mod, g.reshape(1, d), wa, wb, wo)


def _ffn_kernel(te_ref, nu_ref, x_ref, wg_ref, wu_ref, wd_ref, o_ref, xb, acc):
    g = pl.program_id(0)
    j = pl.program_id(1)
    nj = pl.num_programs(1)
    half = x_ref.shape[1]
    used = g < nu_ref[0]

    @pl.when(used & (j == 0))
    def _():
        lo, hi = _unpack_pair(x_ref[...])
        xb[:, :half] = lo.astype(BF16)
        xb[:, half:] = hi.astype(BF16)

    @pl.when(used)
    def _():
        x = xb[...]
        a = _silu(_dot(x, wg_ref[...])) * _dot(x, wu_ref[...])
        part = _dot(a.astype(BF16), wd_ref[...])

        @pl.when(j == 0)
        def _():
            acc[...] = part

        @pl.when(j > 0)
        def _():
            acc[...] += part

    @pl.when(j == nj - 1)
    def _():
        @pl.when(used)
        def _():
            o_ref[...] = _pack_pair(acc[...])

        @pl.when(jnp.logical_not(used))
        def _():
            o_ref[...] = jnp.zeros_like(o_ref)


def _ffn(xu, tile_expert, n_used, wg, wu, wd, tm):
    s, half = xu.shape
    n_e, d, f = wg.shape
    tf = _pick(f, 512, LANES)
    nj = f // tf
    n_tiles = s // tm

    def jj(g, j, nu):
        return jnp.where(g < nu[0], j, nj - 1)

    grid_spec = pltpu.PrefetchScalarGridSpec(
        num_scalar_prefetch=2,
        grid=(n_tiles, nj),
        in_specs=[pl.BlockSpec((tm, half), lambda g, j, te, nu: (g, 0)),
                  pl.BlockSpec((None, d, tf), lambda g, j, te, nu: (te[g], 0, jj(g, j, nu))),
                  pl.BlockSpec((None, d, tf), lambda g, j, te, nu: (te[g], 0, jj(g, j, nu))),
                  pl.BlockSpec((None, tf, d), lambda g, j, te, nu: (te[g], jj(g, j, nu), 0))],
        out_specs=pl.BlockSpec((tm, half), lambda g, j, te, nu: (g, 0)),
        scratch_shapes=[pltpu.VMEM((tm, d), BF16), pltpu.VMEM((tm, d), F32)])
    return pl.pallas_call(
        _ffn_kernel,
        grid_spec=grid_spec,
        out_shape=jax.ShapeDtypeStruct((s, half), jnp.uint32),
        compiler_params=_params("parallel", "arbitrary"),
        name="ffn",
    )(tile_expert, n_used, xu, wg, wu, wd)


def _post_norm_residual(y, x_ref, mod_ref, g_ref, o_ref, ga_i):
    yn = y * lax.rsqrt(jnp.mean(y * y, axis=-1, keepdims=True) + EPS) * g_ref[...]
    o_ref[...] = x_ref[...] + mod_ref[ga_i:ga_i + 1, :] * yn


def _post_kernel(y_ref, x_ref, mod_ref, g_ref, o_ref, *, ga_i):
    lo, hi = _unpack_pair(y_ref[...])
    _post_norm_residual(jnp.concatenate([lo, hi], axis=1), x_ref, mod_ref, g_ref, o_ref, ga_i)


def _post(yu, x, mod, g, row_of, ga_i):
    t, d = x.shape
    tm = _pick(t, 512)
    return pl.pallas_call(
        functools.partial(_post_kernel, ga_i=ga_i),
        grid=(t // tm,),
        in_specs=[pl.BlockSpec((tm, d // 2), lambda i: (i, 0)),
                  pl.BlockSpec((tm, d), lambda i: (i, 0)),
                  _mod_spec(d, tm, row_of),
                  pl.BlockSpec((1, d), lambda i: (0, 0))],
        out_specs=pl.BlockSpec((tm, d), lambda i: (i, 0)),
        out_shape=jax.ShapeDtypeStruct((t, d), F32),
        compiler_params=_params("parallel"),
        name="post",
    )(yu, x, mod, g.reshape(1, d))


def _row_copy(src_ref, src_row, dst_ref, dst_row, sem):
    return pltpu.make_async_copy(src_ref.at[pl.ds(src_row, 1), :], dst_ref.at[pl.ds(dst_row, 1), :], sem)


def _dispatch_kernel(s1_ref, s2_ref, h_ref, init_ref, xs_ref, sem, *, tb):
    del init_ref
    base = pl.program_id(0) * tb

    def issue(t, carry):
        _row_copy(h_ref, base + t, xs_ref, s1_ref[0, 0, t], sem).start()
        _row_copy(h_ref, base + t, xs_ref, s2_ref[0, 0, t], sem).start()
        return carry

    lax.fori_loop(0, tb, issue, 0)

    def drain(t, carry):
        _row_copy(h_ref, 0, xs_ref, 0, sem).wait()
        _row_copy(h_ref, 0, xs_ref, 0, sem).wait()
        return carry

    lax.fori_loop(0, tb, drain, 0)


def _dispatch(hu, slot1, slot2, n_slots):
    t, half = hu.shape
    tb = _pick(t, 1024)
    nb = t // tb
    smem = functools.partial(pl.BlockSpec, (1, 1, tb), lambda i: (i, 0, 0), memory_space=pltpu.SMEM)
    return pl.pallas_call(
        functools.partial(_dispatch_kernel, tb=tb),
        grid=(nb,),
        in_specs=[smem(), smem(),
                  pl.BlockSpec(memory_space=pl.ANY),
                  pl.BlockSpec(memory_space=pl.ANY)],
        out_specs=pl.BlockSpec(memory_space=pl.ANY),
        out_shape=jax.ShapeDtypeStruct((n_slots, half), jnp.uint32),
        scratch_shapes=[pltpu.SemaphoreType.DMA(())],
        input_output_aliases={3: 0},
        compiler_params=pltpu.CompilerParams(dimension_semantics=("arbitrary",), has_side_effects=True),
        name="dispatch",
    )(slot1.reshape(nb, 1, tb), slot2.reshape(nb, 1, tb), hu, jnp.zeros((n_slots, half), jnp.uint32))


def _combine_kernel(s1_ref, s2_ref, info_ref, x_ref, mod_ref, g_ref, ys_ref, o_ref, b1, b2, sem, *,
                    tb, ga_i):
    def issue(t, carry):
        _row_copy(ys_ref, s1_ref[0, 0, t], b1, t, sem).start()
        _row_copy(ys_ref, s2_ref[0, 0, t], b2, t, sem).start()
        return carry

    lax.fori_loop(0, tb, issue, 0)

    def drain(t, carry):
        _row_copy(ys_ref, 0, b1, 0, sem).wait()
        _row_copy(ys_ref, 0, b2, 0, sem).wait()
        return carry

    lax.fori_loop(0, tb, drain, 0)
    info = info_ref[...]
    w1 = info[:, 4:5]
    w2 = info[:, 5:6]
    lo1, hi1 = _unpack_pair(b1[...])
    lo2, hi2 = _unpack_pair(b2[...])
    y = jnp.concatenate([w1 * lo1 + w2 * lo2, w1 * hi1 + w2 * hi2], axis=1)
    _post_norm_residual(y, x_ref, mod_ref, g_ref, o_ref, ga_i)


def _combine(ys, slot1, slot2, info, x, mod, g, row_of, ga_i):
    t, d = x.shape
    half = d // 2
    tb = _pick(t, 512)
    nb = t // tb
    smem = functools.partial(pl.BlockSpec, (1, 1, tb), lambda i: (i, 0, 0), memory_space=pltpu.SMEM)
    return pl.pallas_call(
        functools.partial(_combine_kernel, tb=tb, ga_i=ga_i),
        grid=(nb,),
        in_specs=[smem(), smem(),
                  pl.BlockSpec((tb, LANES), lambda i: (i, 0)),
                  pl.BlockSpec((tb, d), lambda i: (i, 0)),
                  _mod_spec(d, tb, row_of),
                  pl.BlockSpec((1, d), lambda i: (0, 0)),
                  pl.BlockSpec(memory_space=pl.ANY)],
        out_specs=pl.BlockSpec((tb, d), lambda i: (i, 0)),
        out_shape=jax.ShapeDtypeStruct((t, d), F32),
        scratch_shapes=[pltpu.VMEM((tb, half), jnp.uint32), pltpu.VMEM((tb, half), jnp.uint32),
                        pltpu.SemaphoreType.DMA(())],
        compiler_params=_params("arbitrary"),
        name="combine",
    )(slot1.reshape(nb, 1, tb), slot2.reshape(nb, 1, tb), info, x, mod, g.reshape(1, d), ys)


def _route_plan(info, counts, n_exp, tm, n_tiles):
    cnt = counts[0, :n_exp].astype(jnp.int32)
    tiles = (cnt + tm - 1) // tm
    ends = jnp.cumsum(tiles)
    base = (ends - tiles) * tm
    i1 = info[:, 0].astype(jnp.int32)
    i2 = info[:, 1].astype(jnp.int32)
    slot1 = base[i1] + info[:, 2].astype(jnp.int32)
    slot2 = base[i2] + info[:, 3].astype(jnp.int32)
    tile_expert = jnp.minimum(jnp.searchsorted(ends, jnp.arange(n_tiles, dtype=jnp.int32), side="right"),
                              n_exp - 1).astype(jnp.int32)
    return slot1, slot2, tile_expert, ends[-1:].astype(jnp.int32)


def kernel(x_prompt, x_sample, state_gla, c, c_ctx, w_ada, b_ada, g_mix_pre, g_mix_post, w_in, w_alpha_up, b_alpha, g_head, w_gla_br, w_fn_br, w_out, g_ffn_pre, g_ffn_post, w_ffn_gate, w_ffn_up, w_ffn_down, w_router, w_exp_gate, w_exp_up, w_exp_down):
    bp, np_, d = x_prompt.shape
    bs, ns_, _ = x_sample.shape
    depth = w_ada.shape[0]
    heads = GLA_HEADS
    qk = w_alpha_up.shape[-1]
    dk = qk // heads
    vw = g_head.shape[-1]
    dv = vw // heads
    fw = w_fn_br.shape[1]
    gw = fw // FN_GROUPS
    n_exp = w_router.shape[-1]
    p_rows = bp * np_
    t = p_rows + bs * ns_
    assert dk == LANES and 2 * GLA_LR <= LANES and p_rows % ns_ == 0 and ns_ % np_ == 0

    def row_of(start):
        return jnp.where(start < p_rows, 0, 1 + (start - p_rows) // ns_)

    col = {"q": 0, "k": qk, "v": 2 * qk, "r": 2 * qk + vw, "f": 2 * qk + 2 * vw,
           "ga": 2 * qk + 2 * vw + fw, "gb": 2 * qk + 2 * vw + fw + d, "alr": 2 * qk + 2 * vw + fw + 2 * d}
    src_alr = 2 * qk + 2 * vw
    n_u = col["alr"] + LANES
    assert col["ga"] % d == 0 and col["gb"] % d == 0 and col["f"] % gw == 0

    n_cond = 1 + bs
    cond = jnp.zeros((16, d), F32).at[0].set(c_ctx).at[1:n_cond].set(c)
    mod_all = _ada(cond, w_ada, b_ada)[:, :n_cond].reshape(depth, n_cond, 6, d)

    chan_c, chan_s = _dft_tables(gw, gw ** -0.5)
    chan_tab = jnp.concatenate([chan_c, chan_s], axis=1).astype(BF16)
    pos_tabs = {}
    for n in (np_, ns_):
        pc, ps = _dft_tables(n, n ** -0.5)
        pos_tabs[n] = (pc.astype(BF16), (-ps).astype(BF16))

    x = jnp.concatenate([x_prompt.reshape(p_rows, d), x_sample.reshape(bs * ns_, d)], axis=0)
    ctx_states = []
    for l in range(depth):
        mod = mod_all[l]
        w_l = w_in[l]
        w_perm = jnp.concatenate(
            [w_l[:, :src_alr], w_l[:, src_alr + 2 * GLA_LR:], w_l[:, src_alr:src_alr + 2 * GLA_LR],
             jnp.zeros((d, LANES - 2 * GLA_LR), F32)], axis=1).astype(BF16)
        h = _prenorm(x, mod, g_mix_pre[l], row_of, 0, 1, pack=False)
        u = _matmul(h, w_perm)
        wup = w_alpha_up[l].reshape(2, GLA_LR, heads, dk)
        wal = jnp.zeros((heads, LANES, 2 * dk), F32)
        wal = wal.at[:, :GLA_LR, :dk].set(wup[0].transpose(1, 0, 2))
        wal = wal.at[:, GLA_LR:2 * GLA_LR, dk:].set(wup[1].transpose(1, 0, 2)).astype(BF16)
        bal = b_alpha[l].reshape(2, heads, 1, dk).transpose(1, 2, 0, 3).reshape(heads, 1, 2 * dk)
        gh = g_head[l].reshape(heads, 1, dv)
        og_p, s_fin = _gla(u, wal, bal, gh, None, l, row0=0, n_seq=bp, n_rows=np_, col=col, want_state=True)
        og_s, _ = _gla(u, wal, bal, gh, state_gla, l, row0=p_rows, n_seq=bs, n_rows=ns_, col=col,
                       want_state=False)
        ctx_states.append(s_fin)
        xc, xs = _chan_dft(u, chan_tab, col["f"], fw)
        fr_p = _pos_dft(*pos_tabs[np_], xc, xs, row0=0, n_seq=bp, n_rows=np_)
        fr_s = _pos_dft(*pos_tabs[ns_], xc, xs, row0=p_rows, n_seq=bs, n_rows=ns_)
        og = jnp.concatenate([og_p, og_s], axis=0)
        fr = jnp.concatenate([fr_p, fr_s], axis=0)
        x = _mix_out(og, fr, u, x, mod, g_mix_post[l], w_gla_br[l].astype(BF16), w_fn_br[l].astype(BF16),
                     w_out[l].astype(BF16), row_of, col, 2)
        j = l // 2
        if l % 2 == 0:
            hu = _prenorm(x, mod, g_ffn_pre[l], row_of, 3, 4, pack=True)
            tm = _pick(t, 1024)
            n_tiles = t // tm
            yu = _ffn(hu, jnp.zeros((n_tiles,), jnp.int32), jnp.full((1,), n_tiles, jnp.int32),
                      w_ffn_gate[j][None].astype(BF16), w_ffn_up[j][None].astype(BF16),
                      w_ffn_down[j][None].astype(BF16), tm)
            x = _post(yu, x, mod, g_ffn_post[l], row_of, 5)
        else:
            hu, info, counts = _prenorm_route(x, mod, g_ffn_pre[l], w_router[j], row_of, 3, 4)
            tm = _pick(t, 1024)
            n_tiles = (TOP_K * t) // tm + n_exp
            slot1, slot2, tile_expert, n_used = _route_plan(info, counts, n_exp, tm, n_tiles)
            xs_rows = _dispatch(hu, slot1, slot2, n_tiles * tm)
            ys_rows = _ffn(xs_rows, tile_expert, n_used, w_exp_gate[j].astype(BF16),
                           w_exp_up[j].astype(BF16), w_exp_down[j].astype(BF16), tm)
            x = _combine(ys_rows, slot1, slot2, info, x, mod, g_ffn_post[l], row_of, 5)
    y_prompt = x[:p_rows].reshape(bp, np_, d)
    y_sample = x[p_rows:].reshape(bs, ns_, d)
    new_state = jnp.stack(ctx_states, axis=1).astype(x_prompt.dtype)
    return (y_prompt, y_sample, new_state)
```

```python
import functools
import math

import jax
import jax.numpy as jnp
from jax import lax
from jax.experimental import pallas as pl
from jax.experimental.pallas import tpu as pltpu

GLA_HEADS = 4
GLA_LR = 16
GLA_TAU = 16.0
GLA_CHUNK = 64
FN_GROUPS = 4
TOP_K = 2
EPS = 1e-6

LANES = 128
SLAB = 4 * GLA_CHUNK
VMEM_LIMIT = 56 * 1024 * 1024
BF16 = jnp.bfloat16
F32 = jnp.float32


def _pick(n, pref, mult=8):
    t = min(n, pref)
    t -= t % mult
    while t > mult and n % t:
        t -= mult
    assert t > 0 and n % t == 0, (n, pref)
    return t


class _Rows:
    def __init__(self, p_rows, seq_rows):
        self.p_rows, self.seq_rows = p_rows, seq_rows

    def __call__(self, start):
        return jnp.where(start < self.p_rows, 0, 1 + (start - self.p_rows) // self.seq_rows)

    def tile(self, pref):
        return _pick(math.gcd(self.p_rows, self.seq_rows), pref)


def _params(*sem):
    return pltpu.CompilerParams(dimension_semantics=sem, vmem_limit_bytes=VMEM_LIMIT)


def _silu(x):
    return x / (1.0 + jnp.exp(-x))


def _sigmoid(x):
    return 1.0 / (1.0 + jnp.exp(-x))


def _dot(a, b):
    return jnp.dot(a, b, preferred_element_type=F32)


def _dot_nt(a, b):
    return lax.dot_general(a, b, (((1,), (1,)), ((), ())), preferred_element_type=F32)


def _dot_tn(a, b):
    return lax.dot_general(a, b, (((0,), (0,)), ((), ())), preferred_element_type=F32)


def _pack_pair(h):
    half = h.shape[1] // 2
    bits = lax.bitcast_convert_type(h.astype(BF16).astype(F32), jnp.uint32)
    return (bits[:, half:] & jnp.uint32(0xFFFF0000)) | (bits[:, :half] >> 16)


def _unpack_pair(u):
    lo = lax.bitcast_convert_type(u << 16, F32)
    hi = lax.bitcast_convert_type(u & jnp.uint32(0xFFFF0000), F32)
    return lo, hi


def _ada_kernel(c_ref, w_ref, b_ref, o_ref):
    s = _silu(c_ref[...]).astype(BF16)
    o_ref[...] = _dot(s, w_ref[...].astype(BF16)) + b_ref[...]


def _ada(cond, w_ada, b_ada):
    depth, d, n6 = w_ada.shape
    rows = cond.shape[0]
    tn = _pick(n6, 1536, LANES)
    return pl.pallas_call(
        _ada_kernel,
        grid=(depth, n6 // tn),
        in_specs=[pl.BlockSpec((rows, d), lambda l, j: (0, 0)),
                  pl.BlockSpec((None, d, tn), lambda l, j: (l, 0, j)),
                  pl.BlockSpec((None, 1, tn), lambda l, j: (l, 0, j))],
        out_specs=pl.BlockSpec((None, rows, tn), lambda l, j: (l, 0, j)),
        out_shape=jax.ShapeDtypeStruct((depth, rows, n6), F32),
        compiler_params=_params("parallel", "parallel"),
        name="ada",
    )(cond, w_ada, b_ada.reshape(depth, 1, n6))


def _modulated_norm(x_ref, mod_ref, g_ref, sh_i, sc_i):
    x = x_ref[...]
    y = x * lax.rsqrt(jnp.mean(x * x, axis=-1, keepdims=True) + EPS) * g_ref[...]
    return y * (1.0 + mod_ref[sc_i:sc_i + 1, :]) + mod_ref[sh_i:sh_i + 1, :]


def _prenorm_kernel(x_ref, mod_ref, g_ref, o_ref, *, sh_i, sc_i, pack):
    h = _modulated_norm(x_ref, mod_ref, g_ref, sh_i, sc_i)
    o_ref[...] = _pack_pair(h) if pack else h.astype(BF16)


def _route_kernel(x_ref, mod_ref, g_ref, wr_ref, info_ref, cnt_ref, carry, *, sh_i, sc_i, n_exp):
    i = pl.program_id(0)

    @pl.when(i == 0)
    def _():
        carry[...] = jnp.zeros_like(carry)

    h = _modulated_norm(x_ref, mod_ref, g_ref, sh_i, sc_i)
    w = wr_ref[...]
    hh = h.astype(BF16)
    hl = (h - hh.astype(F32)).astype(BF16)
    wh = w.astype(BF16)
    wl = (w - wh.astype(F32)).astype(BF16)
    logits = _dot(hh, wh) + _dot(hl, wh) + _dot(hh, wl)
    tm = logits.shape[0]
    lane = lax.broadcasted_iota(jnp.int32, (tm, LANES), 1).astype(F32)
    neg = jnp.float32(-jnp.inf)
    l1 = jnp.where(lane < n_exp, logits, neg)
    m1 = jnp.max(l1, axis=-1, keepdims=True)
    i1 = jnp.min(jnp.where(l1 == m1, lane, float(LANES)), axis=-1, keepdims=True)
    l2 = jnp.where(lane == i1, neg, l1)
    m2 = jnp.max(l2, axis=-1, keepdims=True)
    i2 = jnp.min(jnp.where(l2 == m2, lane, float(LANES)), axis=-1, keepdims=True)
    e = jnp.exp(m2 - m1)
    w1 = 1.0 / (1.0 + e)
    w2 = e / (1.0 + e)
    sel1 = lane == i1
    sel2 = lane == i2
    a = jnp.where(sel1 | sel2, 1.0, 0.0)
    r = lax.broadcasted_iota(jnp.int32, (tm, tm), 0)
    c = lax.broadcasted_iota(jnp.int32, (tm, tm), 1)
    before = jnp.where(c < r, 1.0, 0.0).astype(BF16)
    pos = _dot(before, a.astype(BF16)) + carry[...]
    p1 = jnp.sum(jnp.where(sel1, pos, 0.0), axis=-1, keepdims=True)
    p2 = jnp.sum(jnp.where(sel2, pos, 0.0), axis=-1, keepdims=True)
    total = carry[...] + jnp.sum(a, axis=0, keepdims=True)
    carry[...] = total
    cnt_ref[...] = jnp.broadcast_to(total, cnt_ref.shape)
    info = jnp.where(lane == 0, i1, 0.0)
    info = jnp.where(lane == 1, i2, info)
    info = jnp.where(lane == 2, p1, info)
    info = jnp.where(lane == 3, p2, info)
    info = jnp.where(lane == 4, w1, info)
    info = jnp.where(lane == 5, w2, info)
    info_ref[...] = info


def _mod_spec(d, tm, row_of):
    return pl.BlockSpec((None, 6, d), lambda i: (row_of(i * tm), 0, 0))


def _prenorm(x, mod, g, row_of, sh_i, sc_i, pack):
    t, d = x.shape
    tm = row_of.tile(512)
    out = jax.ShapeDtypeStruct((t, d // 2), jnp.uint32) if pack else jax.ShapeDtypeStruct((t, d), BF16)
    ow = d // 2 if pack else d
    return pl.pallas_call(
        functools.partial(_prenorm_kernel, sh_i=sh_i, sc_i=sc_i, pack=pack),
        grid=(t // tm,),
        in_specs=[pl.BlockSpec((tm, d), lambda i: (i, 0)),
                  _mod_spec(d, tm, row_of),
                  pl.BlockSpec((1, d), lambda i: (0, 0))],
        out_specs=pl.BlockSpec((tm, ow), lambda i: (i, 0)),
        out_shape=out,
        compiler_params=_params("parallel"),
        name="prenorm",
    )(x, mod, g.reshape(1, d))


def _route(x, mod, g, w_router, row_of, sh_i, sc_i):
    t, d = x.shape
    n_exp = w_router.shape[1]
    tm = row_of.tile(512)
    wr = jnp.zeros((d, LANES), F32).at[:, :n_exp].set(w_router)
    return pl.pallas_call(
        functools.partial(_route_kernel, sh_i=sh_i, sc_i=sc_i, n_exp=n_exp),
        grid=(t // tm,),
        in_specs=[pl.BlockSpec((tm, d), lambda i: (i, 0)),
                  _mod_spec(d, tm, row_of),
                  pl.BlockSpec((1, d), lambda i: (0, 0)),
                  pl.BlockSpec((d, LANES), lambda i: (0, 0))],
        out_specs=[pl.BlockSpec((tm, LANES), lambda i: (i, 0)),
                   pl.BlockSpec((8, LANES), lambda i: (0, 0))],
        out_shape=[jax.ShapeDtypeStruct((t, LANES), F32),
                   jax.ShapeDtypeStruct((8, LANES), F32)],
        scratch_shapes=[pltpu.VMEM((1, LANES), F32)],
        compiler_params=_params("arbitrary"),
        name="route",
    )(x, mod, g.reshape(1, d), wr)


def _in_proj_kernel(x_ref, mod_ref, g_ref, w_ref, o_ref, h_scr, *, sh_i, sc_i):
    @pl.when(pl.program_id(1) == 0)
    def _():
        h_scr[...] = _modulated_norm(x_ref, mod_ref, g_ref, sh_i, sc_i).astype(BF16)

    o_ref[...] = _dot(h_scr[...], w_ref[...]).astype(o_ref.dtype)


def _in_proj(x, mod, g, w, row_of, sh_i, sc_i):
    t, d = x.shape
    n = w.shape[1]
    tm = row_of.tile(1024)
    tn = _pick(n, 1664, LANES)
    return pl.pallas_call(
        functools.partial(_in_proj_kernel, sh_i=sh_i, sc_i=sc_i),
        grid=(t // tm, n // tn),
        in_specs=[pl.BlockSpec((tm, d), lambda i, j: (i, 0)),
                  pl.BlockSpec((None, 6, d), lambda i, j: (row_of(i * tm), 0, 0)),
                  pl.BlockSpec((1, d), lambda i, j: (0, 0)),
                  pl.BlockSpec((d, tn), lambda i, j: (0, j))],
        out_specs=pl.BlockSpec((tm, tn), lambda i, j: (i, j)),
        out_shape=jax.ShapeDtypeStruct((t, n), BF16),
        scratch_shapes=[pltpu.VMEM((tm, d), BF16)],
        compiler_params=_params("parallel", "arbitrary"),
        name="in_proj",
    )(x, mod, g.reshape(1, d), w)


def _gla_kernel(*refs, n_rows, has_s0, has_prev, want_state):
    q_ref, k_ref, v_ref, r_ref, alr_ref, wal_ref, bal_ref, gh_ref = refs[:8]
    pos = 8
    s0_ref = None
    if has_s0:
        s0_ref = refs[pos]
        pos += 1
    if has_prev:
        pos += 1
    og_ref = refs[pos]
    pos += 1
    sfin_ref = None
    if want_state:
        sfin_ref = refs[pos]
        pos += 1
    of_scr, ob_scr, st_scr = refs[pos:pos + 3]
    o_scr = (of_scr, ob_scr)

    dk = q_ref.shape[1]
    n_slab = n_rows // SLAB
    n_chunk = SLAB // GLA_CHUNK
    scale = dk ** -0.5

    ri = lax.broadcasted_iota(jnp.int32, (SLAB, SLAB), 0)
    ci = lax.broadcasted_iota(jnp.int32, (SLAB, SLAB), 1)
    shift = GLA_CHUNK.bit_length() - 1
    same = jnp.right_shift(ri, shift) == jnp.right_shift(ci, shift)
    keep = (same & (ci <= ri), same & (ci >= ri))
    ones = jnp.where(same, 1.0, 0.0).astype(BF16)
    sum_mats = tuple(jnp.concatenate([jnp.where(m, 1.0, 0.0).astype(BF16), ones], axis=0) for m in keep)

    for d in range(2):
        if has_s0:
            st_scr[d] = s0_ref[d].T
        else:
            st_scr[d] = jnp.zeros(st_scr.shape[1:], F32)

    def slab(d, r0):
        rows = pl.ds(r0, SLAB)
        cols = slice(d * dk, (d + 1) * dk)
        z = _dot(alr_ref[rows, :], wal_ref[:, cols]) + bal_ref[:, cols]
        la = (jnp.minimum(z, 0.0) - jnp.log1p(jnp.exp(-jnp.abs(z)))) * (1.0 / GLA_TAU)
        hi = la.astype(BF16)
        rem = la - hi.astype(F32)
        mid = rem.astype(BF16)
        lo = (rem - mid.astype(F32)).astype(BF16)
        ct = _dot(sum_mats[d], jnp.concatenate([hi, mid, lo], axis=1))
        cum = ct[:SLAB, :dk] + ct[:SLAB, dk:2 * dk] + ct[:SLAB, 2 * dk:]
        tot = ct[SLAB:, :dk] + ct[SLAB:, dk:2 * dk] + ct[SLAB:, 2 * dk:]
        q = q_ref[rows, :].astype(F32) * scale
        k = k_ref[rows, :].astype(F32)
        qd = (q * jnp.exp(cum)).astype(BF16)
        ki = (k * jnp.exp(-cum)).astype(BF16)
        ke = (k * jnp.exp(tot - cum)).astype(BF16)
        dec = jnp.exp(tot)
        v = v_ref[rows, :]
        s = jnp.where(keep[d], _dot_nt(qd, ki), 0.0).astype(BF16)
        o = _dot(s, v)
        st = st_scr[d]
        order = range(n_chunk) if d == 0 else range(n_chunk - 1, -1, -1)
        for c in order:
            sl = slice(c * GLA_CHUNK, (c + 1) * GLA_CHUNK)
            o_inter = _dot_nt(qd[sl], st.astype(BF16))
            o_scr[d][pl.ds(r0 + c * GLA_CHUNK, GLA_CHUNK), :] = o[sl] + o_inter
            st = dec[c * GLA_CHUNK:c * GLA_CHUNK + 1, :] * st + _dot_tn(v[sl], ke[sl])
        st_scr[d] = st

    def body(i, carry):
        slab(0, pl.multiple_of(i * SLAB, SLAB))
        slab(1, pl.multiple_of((n_slab - 1 - i) * SLAB, SLAB))
        return carry

    lax.fori_loop(0, n_slab, body, 0, unroll=2 if n_slab % 2 == 0 else 1)

    def finish(i, carry):
        rows = pl.ds(pl.multiple_of(i * SLAB, SLAB), SLAB)
        o = of_scr[rows, :] + ob_scr[rows, :]
        o = o * lax.rsqrt(jnp.mean(o * o, axis=-1, keepdims=True) + EPS) * gh_ref[...]
        og_ref[rows, :] = (o * _silu(r_ref[rows, :].astype(F32))).astype(og_ref.dtype)
        return carry

    lax.fori_loop(0, n_slab, finish, 0)
    if want_state:
        for d in range(2):
            sfin_ref[d] = st_scr[d].T


def _gla(u, wal, bal, gh, s0, layer, prev, *, row0, n_seq, n_rows, col, want_state):
    t = u.shape[0]
    heads, dk2 = wal.shape[0], wal.shape[2]
    dk = dk2 // 2
    dv = gh.shape[2]
    assert row0 % n_rows == 0 and n_rows % SLAB == 0
    rb0 = row0 // n_rows
    has_s0 = s0 is not None
    has_prev = prev is not None

    def at(col_units):
        return lambda b, h: (rb0 + b, col_units + h)

    in_specs = [pl.BlockSpec((n_rows, dk), at(col["q"] // dk)),
                pl.BlockSpec((n_rows, dk), at(col["k"] // dk)),
                pl.BlockSpec((n_rows, dv), at(col["v"] // dv)),
                pl.BlockSpec((n_rows, dv), at(col["r"] // dv)),
                pl.BlockSpec((n_rows, LANES), lambda b, h: (rb0 + b, col["alr"] // LANES)),
                pl.BlockSpec((None, LANES, dk2), lambda b, h: (h, 0, 0)),
                pl.BlockSpec((None, 1, dk2), lambda b, h: (h, 0, 0)),
                pl.BlockSpec((None, 1, dv), lambda b, h: (h, 0, 0))]
    args = [u, u, u, u, u, wal, bal, gh]
    if has_s0:
        in_specs.append(pl.BlockSpec((None, None, 2, None, dk, dv), lambda b, h: (b, layer, 0, h, 0, 0)))
        args.append(s0)
    aliases = {}
    if has_prev:
        aliases = {len(args): 0}
        in_specs.append(pl.BlockSpec(memory_space=pl.ANY))
        args.append(prev)
    out_specs = [pl.BlockSpec((n_rows, dv), lambda b, h: (rb0 + b, h))]
    out_shape = [jax.ShapeDtypeStruct((t, heads * dv), BF16)]
    if want_state:
        out_specs.append(pl.BlockSpec((None, 2, None, dk, dv), lambda b, h: (b, 0, h, 0, 0)))
        out_shape.append(jax.ShapeDtypeStruct((n_seq, 2, heads, dk, dv), F32))
    res = pl.pallas_call(
        functools.partial(_gla_kernel, n_rows=n_rows, has_s0=has_s0, has_prev=has_prev,
                          want_state=want_state),
        grid=(n_seq, heads),
        in_specs=in_specs,
        out_specs=out_specs,
        out_shape=out_shape,
        input_output_aliases=aliases,
        scratch_shapes=[pltpu.VMEM((n_rows, dv), F32), pltpu.VMEM((n_rows, dv), F32),
                        pltpu.VMEM((2, dv, dk), F32)],
        compiler_params=_params("parallel", "parallel"),
        name="gla",
    )(*args)
    return res if want_state else (res[0], None)


def _chan_dft_kernel(x_ref, tab_ref, xc_ref, xs_ref):
    y = _dot(x_ref[...], tab_ref[...])
    w = xc_ref.shape[1]
    xc_ref[...] = y[:, :w].astype(xc_ref.dtype)
    xs_ref[...] = y[:, w:].astype(xs_ref.dtype)


def _chan_dft(u, tab, col_f, width):
    t = u.shape[0]
    gw = tab.shape[0]
    tm = _pick(t, 2048)
    c0 = col_f // gw
    spec_o = pl.BlockSpec((tm, gw), lambda i, g: (i, g))
    return pl.pallas_call(
        _chan_dft_kernel,
        grid=(t // tm, width // gw),
        in_specs=[pl.BlockSpec((tm, gw), lambda i, g: (i, c0 + g)),
                  pl.BlockSpec((gw, 2 * gw), lambda i, g: (0, 0))],
        out_specs=[spec_o, spec_o],
        out_shape=[jax.ShapeDtypeStruct((t, width), BF16)] * 2,
        compiler_params=_params("parallel", "parallel"),
        name="chan_dft",
    )(u, tab)


def _pos_dft_kernel(tc_ref, ts_ref, xc_ref, xs_ref, *rest):
    o_ref = rest[-1]
    o_ref[...] = (_dot(tc_ref[...], xc_ref[...]) + _dot(ts_ref[...], xs_ref[...])).astype(o_ref.dtype)


def _pos_dft(tc, ts, xc, xs, prev, *, row0, n_seq, n_rows):
    t, width = xc.shape
    assert row0 % n_rows == 0
    rb0 = row0 // n_rows
    to = _pick(n_rows, 512)
    nt = n_rows // to
    once = pl.Buffered(1)
    in_specs = [pl.BlockSpec((to, n_rows), lambda b, i: (i, 0)),
                pl.BlockSpec((to, n_rows), lambda b, i: (i, 0)),
                pl.BlockSpec((n_rows, width), lambda b, i: (rb0 + b, 0), pipeline_mode=once),
                pl.BlockSpec((n_rows, width), lambda b, i: (rb0 + b, 0), pipeline_mode=once)]
    args = [tc, ts, xc, xs]
    aliases = {}
    if prev is not None:
        aliases = {len(args): 0}
        in_specs.append(pl.BlockSpec(memory_space=pl.ANY))
        args.append(prev)
    return pl.pallas_call(
        _pos_dft_kernel,
        grid=(n_seq, nt),
        in_specs=in_specs,
        out_specs=pl.BlockSpec((to, width), lambda b, i: ((rb0 + b) * nt + i, 0)),
        out_shape=jax.ShapeDtypeStruct((t, width), BF16),
        input_output_aliases=aliases,
        compiler_params=_params("parallel", "parallel"),
        name="pos_dft",
    )(*args)


def _dft_tables(n, scale):
    j = jnp.arange(n, dtype=jnp.int32)
    ang = ((j[:, None] * j[None, :]) % n).astype(F32) * (2.0 * math.pi / n)
    return jnp.cos(ang) * scale, jnp.sin(ang) * scale


def _mix_out_kernel(og_ref, fr_ref, ga_ref, gb_ref, x_ref, mod_ref, g_ref, wa_ref, wb_ref, wo_ref,
                    o_ref, *, ga_i):
    ba = _dot(og_ref[...], wa_ref[...])
    bb = _dot(fr_ref[...], wb_ref[...])
    merged = _sigmoid(ga_ref[...].astype(F32)) * ba + _sigmoid(gb_ref[...].astype(F32)) * bb
    out = _dot(merged.astype(BF16), wo_ref[...])
    y = out * lax.rsqrt(jnp.mean(out * out, axis=-1, keepdims=True) + EPS) * g_ref[...]
    o_ref[...] = x_ref[...] + mod_ref[ga_i:ga_i + 1, :] * y


def _mix_out(og, fr, u, x, mod, g, wa, wb, wo, row_of, col, ga_i):
    t, d = x.shape
    w = og.shape[1]
    tm = row_of.tile(256)
    once = pl.Buffered(1)
    return pl.pallas_call(
        functools.partial(_mix_out_kernel, ga_i=ga_i),
        grid=(t // tm,),
        in_specs=[pl.BlockSpec((tm, w), lambda i: (i, 0)),
                  pl.BlockSpec((tm, w), lambda i: (i, 0)),
                  pl.BlockSpec((tm, d), lambda i: (i, col["ga"] // d)),
                  pl.BlockSpec((tm, d), lambda i: (i, col["gb"] // d)),
                  pl.BlockSpec((tm, d), lambda i: (i, 0)),
                  _mod_spec(d, tm, row_of),
                  pl.BlockSpec((1, d), lambda i: (0, 0)),
                  pl.BlockSpec((w, d), lambda i: (0, 0), pipeline_mode=once),
                  pl.BlockSpec((w, d), lambda i: (0, 0), pipeline_mode=once),
                  pl.BlockSpec((d, d), lambda i: (0, 0), pipeline_mode=once)],
        out_specs=pl.BlockSpec((tm, d), lambda i: (i, 0)),
        out_shape=jax.ShapeDtypeStruct((t, d), F32),
        compiler_params=_params("parallel"),
        name="mix_out",
    )(og, fr, u, u, x, mod, g.reshape(1, d), wa, wb, wo)


def _ffn_kernel(te_ref, nu_ref, x_ref, wg_ref, wu_ref, wd_ref, o_ref, xb, acc):
    g = pl.program_id(0)
    j = pl.program_id(1)
    nj = pl.num_programs(1)
    half = x_ref.shape[1]
    used = g < nu_ref[0]

    @pl.when(used & (j == 0))
    def _():
        lo, hi = _unpack_pair(x_ref[...])
        xb[:, :half] = lo.astype(BF16)
        xb[:, half:] = hi.astype(BF16)

    @pl.when(used)
    def _():
        x = xb[...]
        a = _silu(_dot(x, wg_ref[...])) * _dot(x, wu_ref[...])
        part = _dot(a.astype(BF16), wd_ref[...])

        @pl.when(j == 0)
        def _():
            acc[...] = part

        @pl.when(j > 0)
        def _():
            acc[...] += part

    @pl.when(j == nj - 1)
    def _():
        @pl.when(used)
        def _():
            o_ref[...] = _pack_pair(acc[...])

        @pl.when(jnp.logical_not(used))
        def _():
            o_ref[...] = jnp.zeros_like(o_ref)


def _ffn(xu, tile_expert, n_used, wg, wu, wd, tm):
    s, half = xu.shape
    n_e, d, f = wg.shape
    tf = _pick(f, 512, LANES)
    nj = f // tf
    n_tiles = s // tm

    def jj(g, j, nu):
        return jnp.where(g < nu[0], j, nj - 1)

    grid_spec = pltpu.PrefetchScalarGridSpec(
        num_scalar_prefetch=2,
        grid=(n_tiles, nj),
        in_specs=[pl.BlockSpec((tm, half), lambda g, j, te, nu: (g, 0)),
                  pl.BlockSpec((None, d, tf), lambda g, j, te, nu: (te[g], 0, jj(g, j, nu))),
                  pl.BlockSpec((None, d, tf), lambda g, j, te, nu: (te[g], 0, jj(g, j, nu))),
                  pl.BlockSpec((None, tf, d), lambda g, j, te, nu: (te[g], jj(g, j, nu), 0))],
        out_specs=pl.BlockSpec((tm, half), lambda g, j, te, nu: (g, 0)),
        scratch_shapes=[pltpu.VMEM((tm, d), BF16), pltpu.VMEM((tm, d), F32)])
    return pl.pallas_call(
        _ffn_kernel,
        grid_spec=grid_spec,
        out_shape=jax.ShapeDtypeStruct((s, half), jnp.uint32),
        compiler_params=_params("parallel", "arbitrary"),
        name="ffn",
    )(tile_expert, n_used, xu, wg, wu, wd)


def _post_norm_residual(y, x_ref, mod_ref, g_ref, o_ref, ga_i):
    yn = y * lax.rsqrt(jnp.mean(y * y, axis=-1, keepdims=True) + EPS) * g_ref[...]
    o_ref[...] = x_ref[...] + mod_ref[ga_i:ga_i + 1, :] * yn


def _post_kernel(y_ref, x_ref, mod_ref, g_ref, o_ref, *, ga_i):
    lo, hi = _unpack_pair(y_ref[...])
    _post_norm_residual(jnp.concatenate([lo, hi], axis=1), x_ref, mod_ref, g_ref, o_ref, ga_i)


def _post(yu, x, mod, g, row_of, ga_i):
    t, d = x.shape
    tm = row_of.tile(512)
    return pl.pallas_call(
        functools.partial(_post_kernel, ga_i=ga_i),
        grid=(t // tm,),
        in_specs=[pl.BlockSpec((tm, d // 2), lambda i: (i, 0)),
                  pl.BlockSpec((tm, d), lambda i: (i, 0)),
                  _mod_spec(d, tm, row_of),
                  pl.BlockSpec((1, d), lambda i: (0, 0))],
        out_specs=pl.BlockSpec((tm, d), lambda i: (i, 0)),
        out_shape=jax.ShapeDtypeStruct((t, d), F32),
        compiler_params=_params("parallel"),
        name="post",
    )(yu, x, mod, g.reshape(1, d))


def _row_copy(src_ref, src_row, dst_ref, dst_row, sem):
    return pltpu.make_async_copy(src_ref.at[pl.ds(src_row, 1), :], dst_ref.at[pl.ds(dst_row, 1), :], sem)


def _dispatch_kernel(s1_ref, s2_ref, x_ref, mod_ref, g_ref, init_ref, xs_ref, buf, sem, *, tb, sh_i, sc_i):
    del init_ref
    i = pl.program_id(0)
    n = pl.num_programs(0)
    cur = i % 2

    def drain(b):
        def wait_pair(t, carry):
            _row_copy(buf.at[b], 0, xs_ref, 0, sem.at[b]).wait()
            _row_copy(buf.at[b], 0, xs_ref, 0, sem.at[b]).wait()
            return carry

        lax.fori_loop(0, tb, wait_pair, 0)

    @pl.when(i >= 2)
    def _():
        drain(cur)

    buf[cur] = _pack_pair(_modulated_norm(x_ref, mod_ref, g_ref, sh_i, sc_i))

    def issue(t, carry):
        _row_copy(buf.at[cur], t, xs_ref, s1_ref[0, 0, t], sem.at[cur]).start()
        _row_copy(buf.at[cur], t, xs_ref, s2_ref[0, 0, t], sem.at[cur]).start()
        return carry

    lax.fori_loop(0, tb, issue, 0)

    @pl.when(i == n - 1)
    def _():
        drain(cur)

        @pl.when(n >= 2)
        def _():
            drain(1 - cur)


def _dispatch(x, mod, g, slot1, slot2, n_slots, row_of, sh_i, sc_i):
    t, d = x.shape
    half = d // 2
    tb = row_of.tile(512)
    nb = t // tb
    smem = functools.partial(pl.BlockSpec, (1, 1, tb), lambda i: (i, 0, 0), memory_space=pltpu.SMEM)
    return pl.pallas_call(
        functools.partial(_dispatch_kernel, tb=tb, sh_i=sh_i, sc_i=sc_i),
        grid=(nb,),
        in_specs=[smem(), smem(),
                  pl.BlockSpec((tb, d), lambda i: (i, 0)),
                  _mod_spec(d, tb, row_of),
                  pl.BlockSpec((1, d), lambda i: (0, 0)),
                  pl.BlockSpec(memory_space=pl.ANY)],
        out_specs=pl.BlockSpec(memory_space=pl.ANY),
        out_shape=jax.ShapeDtypeStruct((n_slots, half), jnp.uint32),
        scratch_shapes=[pltpu.VMEM((2, tb, half), jnp.uint32), pltpu.SemaphoreType.DMA((2,))],
        input_output_aliases={5: 0},
        compiler_params=_params("arbitrary"),
        name="dispatch",
    )(slot1.reshape(nb, 1, tb), slot2.reshape(nb, 1, tb), x, mod, g.reshape(1, d),
      jnp.zeros((n_slots, half), jnp.uint32))


def _combine_kernel(s1_ref, s2_ref, info_ref, x_ref, mod_ref, g_ref, ys_ref, o_ref, b1, b2, sem, *,
                    tb, ga_i):
    def issue(t, carry):
        _row_copy(ys_ref, s1_ref[0, 0, t], b1, t, sem).start()
        _row_copy(ys_ref, s2_ref[0, 0, t], b2, t, sem).start()
        return carry

    lax.fori_loop(0, tb, issue, 0)

    def drain(t, carry):
        _row_copy(ys_ref, 0, b1, 0, sem).wait()
        _row_copy(ys_ref, 0, b2, 0, sem).wait()
        return carry

    lax.fori_loop(0, tb, drain, 0)
    info = info_ref[...]
    w1 = info[:, 4:5]
    w2 = info[:, 5:6]
    lo1, hi1 = _unpack_pair(b1[...])
    lo2, hi2 = _unpack_pair(b2[...])
    y = jnp.concatenate([w1 * lo1 + w2 * lo2, w1 * hi1 + w2 * hi2], axis=1)
    _post_norm_residual(y, x_ref, mod_ref, g_ref, o_ref, ga_i)


def _combine(ys, slot1, slot2, info, x, mod, g, row_of, ga_i):
    t, d = x.shape
    half = d // 2
    tb = row_of.tile(512)
    nb = t // tb
    smem = functools.partial(pl.BlockSpec, (1, 1, tb), lambda i: (i, 0, 0), memory_space=pltpu.SMEM)
    return pl.pallas_call(
        functools.partial(_combine_kernel, tb=tb, ga_i=ga_i),
        grid=(nb,),
        in_specs=[smem(), smem(),
                  pl.BlockSpec((tb, LANES), lambda i: (i, 0)),
                  pl.BlockSpec((tb, d), lambda i: (i, 0)),
                  _mod_spec(d, tb, row_of),
                  pl.BlockSpec((1, d), lambda i: (0, 0)),
                  pl.BlockSpec(memory_space=pl.ANY)],
        out_specs=pl.BlockSpec((tb, d), lambda i: (i, 0)),
        out_shape=jax.ShapeDtypeStruct((t, d), F32),
        scratch_shapes=[pltpu.VMEM((tb, half), jnp.uint32), pltpu.VMEM((tb, half), jnp.uint32),
                        pltpu.SemaphoreType.DMA(())],
        compiler_params=_params("arbitrary"),
        name="combine",
    )(slot1.reshape(nb, 1, tb), slot2.reshape(nb, 1, tb), info, x, mod, g.reshape(1, d), ys)


def _route_plan(info, counts, n_exp, tm, n_tiles):
    cnt = counts[0, :n_exp].astype(jnp.int32)
    tiles = (cnt + tm - 1) // tm
    ends = jnp.cumsum(tiles)
    base = (ends - tiles) * tm
    i1 = info[:, 0].astype(jnp.int32)
    i2 = info[:, 1].astype(jnp.int32)
    slot1 = base[i1] + info[:, 2].astype(jnp.int32)
    slot2 = base[i2] + info[:, 3].astype(jnp.int32)
    tile_ids = jnp.arange(n_tiles, dtype=jnp.int32)
    tile_expert = jnp.minimum(jnp.sum((ends[None, :] <= tile_ids[:, None]).astype(jnp.int32), axis=1), n_exp - 1)
    return slot1, slot2, tile_expert, ends[-1:].astype(jnp.int32)


def kernel(x_prompt, x_sample, state_gla, c, c_ctx, w_ada, b_ada, g_mix_pre, g_mix_post, w_in, w_alpha_up, b_alpha, g_head, w_gla_br, w_fn_br, w_out, g_ffn_pre, g_ffn_post, w_ffn_gate, w_ffn_up, w_ffn_down, w_router, w_exp_gate, w_exp_up, w_exp_down):
    bp, np_, d = x_prompt.shape
    bs, ns_, _ = x_sample.shape
    depth = w_ada.shape[0]
    heads = GLA_HEADS
    qk = w_alpha_up.shape[-1]
    dk = qk // heads
    vw = g_head.shape[-1]
    dv = vw // heads
    fw = w_fn_br.shape[1]
    gw = fw // FN_GROUPS
    n_exp = w_router.shape[-1]
    p_rows = bp * np_
    t = p_rows + bs * ns_
    assert dk == LANES and 2 * GLA_LR <= LANES and p_rows % ns_ == 0 and ns_ % np_ == 0

    row_of = _Rows(p_rows, ns_)

    col = {"q": 0, "k": qk, "v": 2 * qk, "r": 2 * qk + vw, "f": 2 * qk + 2 * vw,
           "ga": 2 * qk + 2 * vw + fw, "gb": 2 * qk + 2 * vw + fw + d, "alr": 2 * qk + 2 * vw + fw + 2 * d}
    src_alr = 2 * qk + 2 * vw
    n_u = col["alr"] + LANES
    assert col["ga"] % d == 0 and col["gb"] % d == 0 and col["f"] % gw == 0

    n_cond = 1 + bs
    cond = jnp.zeros((16, d), F32).at[0].set(c_ctx).at[1:n_cond].set(c)
    mod_all = _ada(cond, w_ada, b_ada)[:, :n_cond].reshape(depth, n_cond, 6, d)

    chan_c, chan_s = _dft_tables(gw, gw ** -0.5)
    chan_tab = jnp.concatenate([chan_c, chan_s], axis=1).astype(BF16)
    pos_tabs = {}
    for n in (np_, ns_):
        pc, ps = _dft_tables(n, n ** -0.5)
        pos_tabs[n] = (pc.astype(BF16), (-ps).astype(BF16))

    x = jnp.concatenate([x_prompt.reshape(p_rows, d), x_sample.reshape(bs * ns_, d)], axis=0)
    ctx_states = []
    for l in range(depth):
        mod = mod_all[l]
        w_l = w_in[l]
        w_perm = jnp.concatenate(
            [w_l[:, :src_alr], w_l[:, src_alr + 2 * GLA_LR:], w_l[:, src_alr:src_alr + 2 * GLA_LR],
             jnp.zeros((d, LANES - 2 * GLA_LR), F32)], axis=1).astype(BF16)
        u = _in_proj(x, mod, g_mix_pre[l], w_perm, row_of, 0, 1)
        wup = w_alpha_up[l].reshape(2, GLA_LR, heads, dk)
        wal = jnp.zeros((heads, LANES, 2 * dk), F32)
        wal = wal.at[:, :GLA_LR, :dk].set(wup[0].transpose(1, 0, 2))
        wal = wal.at[:, GLA_LR:2 * GLA_LR, dk:].set(wup[1].transpose(1, 0, 2)).astype(BF16)
        bal = b_alpha[l].reshape(2, heads, 1, dk).transpose(1, 2, 0, 3).reshape(heads, 1, 2 * dk)
        gh = g_head[l].reshape(heads, 1, dv)
        og, s_fin = _gla(u, wal, bal, gh, None, l, jnp.zeros((t, vw), BF16), row0=0, n_seq=bp, n_rows=np_,
                         col=col, want_state=True)
        og, _ = _gla(u, wal, bal, gh, state_gla, l, og, row0=p_rows, n_seq=bs, n_rows=ns_, col=col,
                     want_state=False)
        ctx_states.append(s_fin)
        xc, xs = _chan_dft(u, chan_tab, col["f"], fw)
        fr = _pos_dft(*pos_tabs[np_], xc, xs, jnp.zeros((t, fw), BF16), row0=0, n_seq=bp, n_rows=np_)
        fr = _pos_dft(*pos_tabs[ns_], xc, xs, fr, row0=p_rows, n_seq=bs, n_rows=ns_)
        x = _mix_out(og, fr, u, x, mod, g_mix_post[l], w_gla_br[l].astype(BF16), w_fn_br[l].astype(BF16),
                     w_out[l].astype(BF16), row_of, col, 2)
        j = l // 2
        if l % 2 == 0:
            hu = _prenorm(x, mod, g_ffn_pre[l], row_of, 3, 4, pack=True)
            tm = _pick(t, 1024)
            n_tiles = t // tm
            yu = _ffn(hu, jnp.zeros((n_tiles,), jnp.int32), jnp.full((1,), n_tiles, jnp.int32),
                      w_ffn_gate[j][None].astype(BF16), w_ffn_up[j][None].astype(BF16),
                      w_ffn_down[j][None].astype(BF16), tm)
            x = _post(yu, x, mod, g_ffn_post[l], row_of, 5)
        else:
            info, counts = _route(x, mod, g_ffn_pre[l], w_router[j], row_of, 3, 4)
            tm = _pick(t, 1024)
            n_tiles = (TOP_K * t) // tm + n_exp
            slot1, slot2, tile_expert, n_used = _route_plan(info, counts, n_exp, tm, n_tiles)
            xs_rows = _dispatch(x, mod, g_ffn_pre[l], slot1, slot2, n_tiles * tm, row_of, 3, 4)
            ys_rows = _ffn(xs_rows, tile_expert, n_used, w_exp_gate[j].astype(BF16),
                           w_exp_up[j].astype(BF16), w_exp_down[j].astype(BF16), tm)
            x = _combine(ys_rows, slot1, slot2, info, x, mod, g_ffn_post[l], row_of, 5)
    y_prompt = x[:p_rows].reshape(bp, np_, d)
    y_sample = x[p_rows:].reshape(bs, ns_, d)
    new_state = jnp.stack(ctx_states, axis=1).astype(x_prompt.dtype)
    return (y_prompt, y_sample, new_state)
```

```python
import functools
import math

import jax
import jax.numpy as jnp
from jax import lax
from jax.experimental import pallas as pl
from jax.experimental.pallas import tpu as pltpu

GLA_HEADS = 4
GLA_LR = 16
GLA_TAU = 16.0
GLA_CHUNK = 64
FN_GROUPS = 4
TOP_K = 2
EPS = 1e-6

LANES = 128
SLAB = 4 * GLA_CHUNK
VMEM_LIMIT = 56 * 1024 * 1024
BF16 = jnp.bfloat16
F32 = jnp.float32


def _pick(n, pref, mult=8):
    t = min(n, pref)
    t -= t % mult
    while t > mult and n % t:
        t -= mult
    assert t > 0 and n % t == 0, (n, pref)
    return t


class _Rows:
    def __init__(self, p_rows, seq_rows):
        self.p_rows, self.seq_rows = p_rows, seq_rows

    def __call__(self, start):
        return jnp.where(start < self.p_rows, 0, 1 + (start - self.p_rows) // self.seq_rows)

    def tile(self, pref):
        return _pick(math.gcd(self.p_rows, self.seq_rows), pref)


def _params(*sem):
    return pltpu.CompilerParams(dimension_semantics=sem, vmem_limit_bytes=VMEM_LIMIT)


def _silu(x):
    return x / (1.0 + jnp.exp(-x))


def _sigmoid(x):
    return 1.0 / (1.0 + jnp.exp(-x))


def _dot(a, b):
    return jnp.dot(a, b, preferred_element_type=F32)


def _dot_nt(a, b):
    return lax.dot_general(a, b, (((1,), (1,)), ((), ())), preferred_element_type=F32)


def _dot_tn(a, b):
    return lax.dot_general(a, b, (((0,), (0,)), ((), ())), preferred_element_type=F32)


def _pack_pair(h):
    half = h.shape[1] // 2
    bits = lax.bitcast_convert_type(h.astype(BF16).astype(F32), jnp.uint32)
    return (bits[:, half:] & jnp.uint32(0xFFFF0000)) | (bits[:, :half] >> 16)


def _unpack_pair(u):
    lo = lax.bitcast_convert_type(u << 16, F32)
    hi = lax.bitcast_convert_type(u & jnp.uint32(0xFFFF0000), F32)
    return lo, hi


def _ada_kernel(c_ref, w_ref, b_ref, o_ref):
    s = _silu(c_ref[...]).astype(BF16)
    o_ref[...] = _dot(s, w_ref[...].astype(BF16)) + b_ref[...]


def _ada(cond, w_ada, b_ada):
    depth, d, n6 = w_ada.shape
    rows = cond.shape[0]
    tn = _pick(n6, 1536, LANES)
    return pl.pallas_call(
        _ada_kernel,
        grid=(depth, n6 // tn),
        in_specs=[pl.BlockSpec((rows, d), lambda l, j: (0, 0)),
                  pl.BlockSpec((None, d, tn), lambda l, j: (l, 0, j)),
                  pl.BlockSpec((None, 1, tn), lambda l, j: (l, 0, j))],
        out_specs=pl.BlockSpec((None, rows, tn), lambda l, j: (l, 0, j)),
        out_shape=jax.ShapeDtypeStruct((depth, rows, n6), F32),
        compiler_params=_params("parallel", "parallel"),
        name="ada",
    )(cond, w_ada, b_ada.reshape(depth, 1, n6))


def _modulated_norm(x_ref, mod_ref, g_ref, sh_i, sc_i):
    x = x_ref[...]
    y = x * lax.rsqrt(jnp.mean(x * x, axis=-1, keepdims=True) + EPS) * g_ref[...]
    return y * (1.0 + mod_ref[sc_i:sc_i + 1, :]) + mod_ref[sh_i:sh_i + 1, :]


def _prenorm_kernel(x_ref, mod_ref, g_ref, o_ref, *, sh_i, sc_i, pack):
    h = _modulated_norm(x_ref, mod_ref, g_ref, sh_i, sc_i)
    o_ref[...] = _pack_pair(h) if pack else h.astype(BF16)


def _route_kernel(x_ref, mod_ref, g_ref, wr_ref, info_ref, cnt_ref, carry, *, sh_i, sc_i, n_exp):
    i = pl.program_id(0)

    @pl.when(i == 0)
    def _():
        carry[...] = jnp.zeros_like(carry)

    h = _modulated_norm(x_ref, mod_ref, g_ref, sh_i, sc_i)
    w = wr_ref[...]
    hh = h.astype(BF16)
    hl = (h - hh.astype(F32)).astype(BF16)
    wh = w.astype(BF16)
    wl = (w - wh.astype(F32)).astype(BF16)
    logits = _dot(hh, wh) + _dot(hl, wh) + _dot(hh, wl)
    tm = logits.shape[0]
    lane = lax.broadcasted_iota(jnp.int32, (tm, LANES), 1).astype(F32)
    neg = jnp.float32(-jnp.inf)
    l1 = jnp.where(lane < n_exp, logits, neg)
    m1 = jnp.max(l1, axis=-1, keepdims=True)
    i1 = jnp.min(jnp.where(l1 == m1, lane, float(LANES)), axis=-1, keepdims=True)
    l2 = jnp.where(lane == i1, neg, l1)
    m2 = jnp.max(l2, axis=-1, keepdims=True)
    i2 = jnp.min(jnp.where(l2 == m2, lane, float(LANES)), axis=-1, keepdims=True)
    e = jnp.exp(m2 - m1)
    w1 = 1.0 / (1.0 + e)
    w2 = e / (1.0 + e)
    sel1 = lane == i1
    sel2 = lane == i2
    a = jnp.where(sel1 | sel2, 1.0, 0.0)
    r = lax.broadcasted_iota(jnp.int32, (tm, tm), 0)
    c = lax.broadcasted_iota(jnp.int32, (tm, tm), 1)
    before = jnp.where(c < r, 1.0, 0.0).astype(BF16)
    pos = _dot(before, a.astype(BF16)) + carry[...]
    p1 = jnp.sum(jnp.where(sel1, pos, 0.0), axis=-1, keepdims=True)
    p2 = jnp.sum(jnp.where(sel2, pos, 0.0), axis=-1, keepdims=True)
    total = carry[...] + jnp.sum(a, axis=0, keepdims=True)
    carry[...] = total
    cnt_ref[...] = jnp.broadcast_to(total, cnt_ref.shape)
    info = jnp.where(lane == 0, i1, 0.0)
    info = jnp.where(lane == 1, i2, info)
    info = jnp.where(lane == 2, p1, info)
    info = jnp.where(lane == 3, p2, info)
    info = jnp.where(lane == 4, w1, info)
    info = jnp.where(lane == 5, w2, info)
    info_ref[...] = info


def _mod_spec(d, tm, row_of):
    return pl.BlockSpec((None, 6, d), lambda i: (row_of(i * tm), 0, 0))


def _prenorm(x, mod, g, row_of, sh_i, sc_i, pack):
    t, d = x.shape
    tm = row_of.tile(512)
    out = jax.ShapeDtypeStruct((t, d // 2), jnp.uint32) if pack else jax.ShapeDtypeStruct((t, d), BF16)
    ow = d // 2 if pack else d
    return pl.pallas_call(
        functools.partial(_prenorm_kernel, sh_i=sh_i, sc_i=sc_i, pack=pack),
        grid=(t // tm,),
        in_specs=[pl.BlockSpec((tm, d), lambda i: (i, 0)),
                  _mod_spec(d, tm, row_of),
                  pl.BlockSpec((1, d), lambda i: (0, 0))],
        out_specs=pl.BlockSpec((tm, ow), lambda i: (i, 0)),
        out_shape=out,
        compiler_params=_params("parallel"),
        name="prenorm",
    )(x, mod, g.reshape(1, d))


def _route(x, mod, g, w_router, row_of, sh_i, sc_i):
    t, d = x.shape
    n_exp = w_router.shape[1]
    tm = row_of.tile(512)
    wr = jnp.zeros((d, LANES), F32).at[:, :n_exp].set(w_router)
    return pl.pallas_call(
        functools.partial(_route_kernel, sh_i=sh_i, sc_i=sc_i, n_exp=n_exp),
        grid=(t // tm,),
        in_specs=[pl.BlockSpec((tm, d), lambda i: (i, 0)),
                  _mod_spec(d, tm, row_of),
                  pl.BlockSpec((1, d), lambda i: (0, 0)),
                  pl.BlockSpec((d, LANES), lambda i: (0, 0))],
        out_specs=[pl.BlockSpec((tm, LANES), lambda i: (i, 0)),
                   pl.BlockSpec((8, LANES), lambda i: (0, 0))],
        out_shape=[jax.ShapeDtypeStruct((t, LANES), F32),
                   jax.ShapeDtypeStruct((8, LANES), F32)],
        scratch_shapes=[pltpu.VMEM((1, LANES), F32)],
        compiler_params=_params("arbitrary"),
        name="route",
    )(x, mod, g.reshape(1, d), wr)


def _in_proj_kernel(x_ref, mod_ref, g_ref, w_ref, o_ref, h_scr, *, sh_i, sc_i):
    @pl.when(pl.program_id(1) == 0)
    def _():
        h_scr[...] = _modulated_norm(x_ref, mod_ref, g_ref, sh_i, sc_i).astype(BF16)

    o_ref[...] = _dot(h_scr[...], w_ref[...]).astype(o_ref.dtype)


def _in_proj(x, mod, g, w, row_of, sh_i, sc_i):
    t, d = x.shape
    n = w.shape[1]
    tm = row_of.tile(1024)
    tn = _pick(n, 1664, LANES)
    return pl.pallas_call(
        functools.partial(_in_proj_kernel, sh_i=sh_i, sc_i=sc_i),
        grid=(t // tm, n // tn),
        in_specs=[pl.BlockSpec((tm, d), lambda i, j: (i, 0)),
                  pl.BlockSpec((None, 6, d), lambda i, j: (row_of(i * tm), 0, 0)),
                  pl.BlockSpec((1, d), lambda i, j: (0, 0)),
                  pl.BlockSpec((d, tn), lambda i, j: (0, j))],
        out_specs=pl.BlockSpec((tm, tn), lambda i, j: (i, j)),
        out_shape=jax.ShapeDtypeStruct((t, n), BF16),
        scratch_shapes=[pltpu.VMEM((tm, d), BF16)],
        compiler_params=_params("parallel", "arbitrary"),
        name="in_proj",
    )(x, mod, g.reshape(1, d), w)


def _gla_kernel(*refs, n_rows, has_s0, has_prev, want_state):
    q_ref, k_ref, v_ref, r_ref, alr_ref, wal_ref, bal_ref, gh_ref = refs[:8]
    pos = 8
    s0_ref = None
    if has_s0:
        s0_ref = refs[pos]
        pos += 1
    if has_prev:
        pos += 1
    og_ref = refs[pos]
    pos += 1
    sfin_ref = None
    if want_state:
        sfin_ref = refs[pos]
        pos += 1
    of_scr, ob_scr, st_scr = refs[pos:pos + 3]
    o_scr = (of_scr, ob_scr)

    dk = q_ref.shape[1]
    n_slab = n_rows // SLAB
    n_chunk = SLAB // GLA_CHUNK
    scale = dk ** -0.5

    ri = lax.broadcasted_iota(jnp.int32, (SLAB, SLAB), 0)
    ci = lax.broadcasted_iota(jnp.int32, (SLAB, SLAB), 1)
    shift = GLA_CHUNK.bit_length() - 1
    same = jnp.right_shift(ri, shift) == jnp.right_shift(ci, shift)
    keep = (same & (ci <= ri), same & (ci >= ri))
    sum_mats = tuple(jnp.where(m, 1.0, 0.0).astype(BF16) for m in keep)

    for d in range(2):
        if has_s0:
            st_scr[d] = s0_ref[d].T
        else:
            st_scr[d] = jnp.zeros(st_scr.shape[1:], F32)

    def slab(d, r0):
        rows = pl.ds(r0, SLAB)
        cols = slice(d * dk, (d + 1) * dk)
        z = _dot(alr_ref[rows, :], wal_ref[:, cols]) + bal_ref[:, cols]
        la = (jnp.minimum(z, 0.0) - jnp.log1p(jnp.exp(-jnp.abs(z)))) * (1.0 / GLA_TAU)
        hi = la.astype(BF16)
        rem = la - hi.astype(F32)
        mid = rem.astype(BF16)
        lo = (rem - mid.astype(F32)).astype(BF16)
        ct = _dot(sum_mats[d], jnp.concatenate([hi, mid, lo], axis=1))
        cum = ct[:, :dk] + ct[:, dk:2 * dk] + ct[:, 2 * dk:]
        edge = GLA_CHUNK - 1 if d == 0 else 0
        tot_rows = [cum[c * GLA_CHUNK + edge:c * GLA_CHUNK + edge + 1, :] for c in range(n_chunk)]
        tot = jnp.concatenate([jnp.broadcast_to(r, (GLA_CHUNK, dk)) for r in tot_rows], axis=0)
        q = q_ref[rows, :].astype(F32) * scale
        k = k_ref[rows, :].astype(F32)
        qd = (q * jnp.exp(cum)).astype(BF16)
        ki = (k * jnp.exp(-cum)).astype(BF16)
        ke = (k * jnp.exp(tot - cum)).astype(BF16)
        dec = [jnp.exp(r) for r in tot_rows]
        v = v_ref[rows, :]
        s = jnp.where(keep[d], _dot_nt(qd, ki), 0.0).astype(BF16)
        o = _dot(s, v)
        st = st_scr[d]
        order = range(n_chunk) if d == 0 else range(n_chunk - 1, -1, -1)
        for c in order:
            sl = slice(c * GLA_CHUNK, (c + 1) * GLA_CHUNK)
            o_inter = _dot_nt(qd[sl], st.astype(BF16))
            o_scr[d][pl.ds(r0 + c * GLA_CHUNK, GLA_CHUNK), :] = o[sl] + o_inter
            st = dec[c] * st + _dot_tn(v[sl], ke[sl])
        st_scr[d] = st

    def body(i, carry):
        slab(0, pl.multiple_of(i * SLAB, SLAB))
        slab(1, pl.multiple_of((n_slab - 1 - i) * SLAB, SLAB))
        return carry

    lax.fori_loop(0, n_slab, body, 0, unroll=math.gcd(n_slab, 4))

    def finish(i, carry):
        rows = pl.ds(pl.multiple_of(i * SLAB, SLAB), SLAB)
        o = of_scr[rows, :] + ob_scr[rows, :]
        o = o * lax.rsqrt(jnp.mean(o * o, axis=-1, keepdims=True) + EPS) * gh_ref[...]
        og_ref[rows, :] = (o * _silu(r_ref[rows, :].astype(F32))).astype(og_ref.dtype)
        return carry

    lax.fori_loop(0, n_slab, finish, 0)
    if want_state:
        for d in range(2):
            sfin_ref[d] = st_scr[d].T


def _gla(u, wal, bal, gh, s0, layer, prev, *, row0, n_seq, n_rows, col, want_state):
    t = u.shape[0]
    heads, dk2 = wal.shape[0], wal.shape[2]
    dk = dk2 // 2
    dv = gh.shape[2]
    assert row0 % n_rows == 0 and n_rows % SLAB == 0
    rb0 = row0 // n_rows
    has_s0 = s0 is not None
    has_prev = prev is not None

    def at(col_units):
        return lambda b, h: (rb0 + b, col_units + h)

    in_specs = [pl.BlockSpec((n_rows, dk), at(col["q"] // dk)),
                pl.BlockSpec((n_rows, dk), at(col["k"] // dk)),
                pl.BlockSpec((n_rows, dv), at(col["v"] // dv)),
                pl.BlockSpec((n_rows, dv), at(col["r"] // dv)),
                pl.BlockSpec((n_rows, LANES), lambda b, h: (rb0 + b, col["alr"] // LANES)),
                pl.BlockSpec((None, LANES, dk2), lambda b, h: (h, 0, 0)),
                pl.BlockSpec((None, 1, dk2), lambda b, h: (h, 0, 0)),
                pl.BlockSpec((None, 1, dv), lambda b, h: (h, 0, 0))]
    args = [u, u, u, u, u, wal, bal, gh]
    if has_s0:
        in_specs.append(pl.BlockSpec((None, None, 2, None, dk, dv), lambda b, h: (b, layer, 0, h, 0, 0)))
        args.append(s0)
    aliases = {}
    if has_prev:
        aliases = {len(args): 0}
        in_specs.append(pl.BlockSpec(memory_space=pl.ANY))
        args.append(prev)
    out_specs = [pl.BlockSpec((n_rows, dv), lambda b, h: (rb0 + b, h))]
    out_shape = [jax.ShapeDtypeStruct((t, heads * dv), BF16)]
    if want_state:
        out_specs.append(pl.BlockSpec((None, 2, None, dk, dv), lambda b, h: (b, 0, h, 0, 0)))
        out_shape.append(jax.ShapeDtypeStruct((n_seq, 2, heads, dk, dv), F32))
    res = pl.pallas_call(
        functools.partial(_gla_kernel, n_rows=n_rows, has_s0=has_s0, has_prev=has_prev,
                          want_state=want_state),
        grid=(n_seq, heads),
        in_specs=in_specs,
        out_specs=out_specs,
        out_shape=out_shape,
        input_output_aliases=aliases,
        scratch_shapes=[pltpu.VMEM((n_rows, dv), F32), pltpu.VMEM((n_rows, dv), F32),
                        pltpu.VMEM((2, dv, dk), F32)],
        compiler_params=_params("parallel", "parallel"),
        name="gla",
    )(*args)
    return res if want_state else (res[0], None)


def _chan_dft_kernel(x_ref, tab_ref, xc_ref, xs_ref):
    y = _dot(x_ref[...], tab_ref[...])
    w = xc_ref.shape[1]
    xc_ref[...] = y[:, :w].astype(xc_ref.dtype)
    xs_ref[...] = y[:, w:].astype(xs_ref.dtype)


def _chan_dft(u, tab, col_f, width):
    t = u.shape[0]
    gw = tab.shape[0]
    tm = _pick(t, 2048)
    c0 = col_f // gw
    spec_o = pl.BlockSpec((tm, gw), lambda i, g: (i, g))
    return pl.pallas_call(
        _chan_dft_kernel,
        grid=(t // tm, width // gw),
        in_specs=[pl.BlockSpec((tm, gw), lambda i, g: (i, c0 + g)),
                  pl.BlockSpec((gw, 2 * gw), lambda i, g: (0, 0))],
        out_specs=[spec_o, spec_o],
        out_shape=[jax.ShapeDtypeStruct((t, width), BF16)] * 2,
        compiler_params=_params("parallel", "parallel"),
        name="chan_dft",
    )(u, tab)


def _pos_dft_kernel(tc_ref, ts_ref, xc_ref, xs_ref, *rest):
    o_ref = rest[-1]
    o_ref[...] = (_dot(tc_ref[...], xc_ref[...]) + _dot(ts_ref[...], xs_ref[...])).astype(o_ref.dtype)


def _pos_dft(tc, ts, xc, xs, prev, *, row0, n_seq, n_rows):
    t, width = xc.shape
    assert row0 % n_rows == 0
    rb0 = row0 // n_rows
    to = _pick(n_rows, 512)
    nt = n_rows // to
    once = pl.Buffered(1)
    in_specs = [pl.BlockSpec((to, n_rows), lambda b, i: (i, 0)),
                pl.BlockSpec((to, n_rows), lambda b, i: (i, 0)),
                pl.BlockSpec((n_rows, width), lambda b, i: (rb0 + b, 0), pipeline_mode=once),
                pl.BlockSpec((n_rows, width), lambda b, i: (rb0 + b, 0), pipeline_mode=once)]
    args = [tc, ts, xc, xs]
    aliases = {}
    if prev is not None:
        aliases = {len(args): 0}
        in_specs.append(pl.BlockSpec(memory_space=pl.ANY))
        args.append(prev)
    return pl.pallas_call(
        _pos_dft_kernel,
        grid=(n_seq, nt),
        in_specs=in_specs,
        out_specs=pl.BlockSpec((to, width), lambda b, i: ((rb0 + b) * nt + i, 0)),
        out_shape=jax.ShapeDtypeStruct((t, width), BF16),
        input_output_aliases=aliases,
        compiler_params=_params("parallel", "parallel"),
        name="pos_dft",
    )(*args)


def _dft_tables(n, scale):
    j = jnp.arange(n, dtype=jnp.int32)
    ang = ((j[:, None] * j[None, :]) % n).astype(F32) * (2.0 * math.pi / n)
    return jnp.cos(ang) * scale, jnp.sin(ang) * scale


def _mix_out_kernel(og_ref, fr_ref, ga_ref, gb_ref, x_ref, mod_ref, g_ref, wa_ref, wb_ref, wo_ref,
                    o_ref, *, ga_i):
    ba = _dot(og_ref[...], wa_ref[...])
    bb = _dot(fr_ref[...], wb_ref[...])
    merged = _sigmoid(ga_ref[...].astype(F32)) * ba + _sigmoid(gb_ref[...].astype(F32)) * bb
    out = _dot(merged.astype(BF16), wo_ref[...])
    y = out * lax.rsqrt(jnp.mean(out * out, axis=-1, keepdims=True) + EPS) * g_ref[...]
    o_ref[...] = x_ref[...] + mod_ref[ga_i:ga_i + 1, :] * y


def _mix_out(og, fr, u, x, mod, g, wa, wb, wo, row_of, col, ga_i):
    t, d = x.shape
    w = og.shape[1]
    tm = row_of.tile(512)
    once = pl.Buffered(1)
    return pl.pallas_call(
        functools.partial(_mix_out_kernel, ga_i=ga_i),
        grid=(t // tm,),
        in_specs=[pl.BlockSpec((tm, w), lambda i: (i, 0)),
                  pl.BlockSpec((tm, w), lambda i: (i, 0)),
                  pl.BlockSpec((tm, d), lambda i: (i, col["ga"] // d)),
                  pl.BlockSpec((tm, d), lambda i: (i, col["gb"] // d)),
                  pl.BlockSpec((tm, d), lambda i: (i, 0)),
                  _mod_spec(d, tm, row_of),
                  pl.BlockSpec((1, d), lambda i: (0, 0)),
                  pl.BlockSpec((w, d), lambda i: (0, 0), pipeline_mode=once),
                  pl.BlockSpec((w, d), lambda i: (0, 0), pipeline_mode=once),
                  pl.BlockSpec((d, d), lambda i: (0, 0), pipeline_mode=once)],
        out_specs=pl.BlockSpec((tm, d), lambda i: (i, 0)),
        out_shape=jax.ShapeDtypeStruct((t, d), F32),
        compiler_params=_params("parallel"),
        name="mix_out",
    )(og, fr, u, u, x, mod, g.reshape(1, d), wa, wb, wo)


def _ffn_kernel(te_ref, nu_ref, x_ref, wg_ref, wu_ref, wd_ref, o_ref, xb, acc):
    g = pl.program_id(0)
    j = pl.program_id(1)
    nj = pl.num_programs(1)
    half = x_ref.shape[1]
    used = g < nu_ref[0]

    @pl.when(used & (j == 0))
    def _():
        lo, hi = _unpack_pair(x_ref[...])
        xb[:, :half] = lo.astype(BF16)
        xb[:, half:] = hi.astype(BF16)
        acc[...] = jnp.zeros_like(acc)

    @pl.when(used)
    def _():
        x = xb[...]
        a = _silu(_dot(x, wg_ref[...])) * _dot(x, wu_ref[...])
        acc[...] += _dot(a.astype(BF16), wd_ref[...])

    @pl.when(j == nj - 1)
    def _():
        @pl.when(used)
        def _():
            o_ref[...] = _pack_pair(acc[...])

        @pl.when(jnp.logical_not(used))
        def _():
            o_ref[...] = jnp.zeros_like(o_ref)


def _ffn(xu, tile_expert, n_used, wg, wu, wd, tm):
    s, half = xu.shape
    n_e, d, f = wg.shape
    tf = _pick(f, 512, LANES)
    nj = f // tf
    n_tiles = s // tm

    def jj(g, j, nu):
        return jnp.where(g < nu[0], j, nj - 1)

    grid_spec = pltpu.PrefetchScalarGridSpec(
        num_scalar_prefetch=2,
        grid=(n_tiles, nj),
        in_specs=[pl.BlockSpec((tm, half), lambda g, j, te, nu: (g, 0)),
                  pl.BlockSpec((None, d, tf), lambda g, j, te, nu: (te[g], 0, jj(g, j, nu))),
                  pl.BlockSpec((None, d, tf), lambda g, j, te, nu: (te[g], 0, jj(g, j, nu))),
                  pl.BlockSpec((None, tf, d), lambda g, j, te, nu: (te[g], jj(g, j, nu), 0))],
        out_specs=pl.BlockSpec((tm, half), lambda g, j, te, nu: (g, 0)),
        scratch_shapes=[pltpu.VMEM((tm, d), BF16), pltpu.VMEM((tm, d), F32)])
    return pl.pallas_call(
        _ffn_kernel,
        grid_spec=grid_spec,
        out_shape=jax.ShapeDtypeStruct((s, half), jnp.uint32),
        compiler_params=_params("parallel", "arbitrary"),
        name="ffn",
    )(tile_expert, n_used, xu, wg, wu, wd)


def _post_norm_residual(y, x_ref, mod_ref, g_ref, o_ref, ga_i):
    yn = y * lax.rsqrt(jnp.mean(y * y, axis=-1, keepdims=True) + EPS) * g_ref[...]
    o_ref[...] = x_ref[...] + mod_ref[ga_i:ga_i + 1, :] * yn


def _post_kernel(y_ref, x_ref, mod_ref, g_ref, o_ref, *, ga_i):
    lo, hi = _unpack_pair(y_ref[...])
    _post_norm_residual(jnp.concatenate([lo, hi], axis=1), x_ref, mod_ref, g_ref, o_ref, ga_i)


def _post(yu, x, mod, g, row_of, ga_i, row0=0, n_rows=None):
    t, d = x.shape
    n_rows = t if n_rows is None else n_rows
    tm = row_of.tile(512)
    b0 = row0 // tm
    return pl.pallas_call(
        functools.partial(_post_kernel, ga_i=ga_i),
        grid=(n_rows // tm,),
        in_specs=[pl.BlockSpec((tm, d // 2), lambda i: (b0 + i, 0)),
                  pl.BlockSpec((tm, d), lambda i: (b0 + i, 0)),
                  pl.BlockSpec((None, 6, d), lambda i: (row_of((b0 + i) * tm), 0, 0)),
                  pl.BlockSpec((1, d), lambda i: (0, 0))],
        out_specs=pl.BlockSpec((tm, d), lambda i: (i, 0)),
        out_shape=jax.ShapeDtypeStruct((n_rows, d), F32),
        compiler_params=_params("parallel"),
        name="post",
    )(yu, x, mod, g.reshape(1, d))


def _row_copy(src_ref, src_row, dst_ref, dst_row, sem):
    return pltpu.make_async_copy(src_ref.at[pl.ds(src_row, 1), :], dst_ref.at[pl.ds(dst_row, 1), :], sem)


def _dispatch_kernel(s1_ref, s2_ref, x_ref, mod_ref, g_ref, init_ref, xs_ref, buf, sem, *, tb, sh_i, sc_i):
    del init_ref
    i = pl.program_id(0)
    n = pl.num_programs(0)
    cur = i % 2

    def drain(b):
        whole = pltpu.make_async_copy(buf.at[b], xs_ref.at[pl.ds(0, tb), :], sem.at[b])
        whole.wait()
        whole.wait()

    @pl.when(i >= 2)
    def _():
        drain(cur)

    buf[cur] = _pack_pair(_modulated_norm(x_ref, mod_ref, g_ref, sh_i, sc_i))

    def issue(t, carry):
        _row_copy(buf.at[cur], t, xs_ref, s1_ref[0, 0, t], sem.at[cur]).start()
        _row_copy(buf.at[cur], t, xs_ref, s2_ref[0, 0, t], sem.at[cur]).start()
        return carry

    lax.fori_loop(0, tb, issue, 0, unroll=8)

    @pl.when(i == n - 1)
    def _():
        drain(cur)

        @pl.when(n >= 2)
        def _():
            drain(1 - cur)


def _dispatch(x, mod, g, slot1, slot2, n_slots, row_of, sh_i, sc_i):
    t, d = x.shape
    half = d // 2
    tb = row_of.tile(512)
    nb = t // tb
    smem = functools.partial(pl.BlockSpec, (1, 1, tb), lambda i: (i, 0, 0), memory_space=pltpu.SMEM)
    return pl.pallas_call(
        functools.partial(_dispatch_kernel, tb=tb, sh_i=sh_i, sc_i=sc_i),
        grid=(nb,),
        in_specs=[smem(), smem(),
                  pl.BlockSpec((tb, d), lambda i: (i, 0)),
                  _mod_spec(d, tb, row_of),
                  pl.BlockSpec((1, d), lambda i: (0, 0)),
                  pl.BlockSpec(memory_space=pl.ANY)],
        out_specs=pl.BlockSpec(memory_space=pl.ANY),
        out_shape=jax.ShapeDtypeStruct((n_slots, half), jnp.uint32),
        scratch_shapes=[pltpu.VMEM((2, tb, half), jnp.uint32), pltpu.SemaphoreType.DMA((2,))],
        input_output_aliases={5: 0},
        compiler_params=_params("arbitrary"),
        name="dispatch",
    )(slot1.reshape(nb, 1, tb), slot2.reshape(nb, 1, tb), x, mod, g.reshape(1, d),
      jnp.zeros((n_slots, half), jnp.uint32))


def _combine_kernel(s1_ref, s2_ref, info_ref, x_ref, mod_ref, g_ref, ys_ref, o_ref, b1, b2, sem, *,
                    tb, ga_i):
    def issue(t, carry):
        _row_copy(ys_ref, s1_ref[0, 0, t], b1, t, sem).start()
        _row_copy(ys_ref, s2_ref[0, 0, t], b2, t, sem).start()
        return carry

    lax.fori_loop(0, tb, issue, 0, unroll=8)

    pltpu.make_async_copy(ys_ref.at[pl.ds(0, tb), :], b1, sem).wait()
    pltpu.make_async_copy(ys_ref.at[pl.ds(0, tb), :], b2, sem).wait()
    info = info_ref[...]
    w1 = info[:, 4:5]
    w2 = info[:, 5:6]
    lo1, hi1 = _unpack_pair(b1[...])
    lo2, hi2 = _unpack_pair(b2[...])
    y = jnp.concatenate([w1 * lo1 + w2 * lo2, w1 * hi1 + w2 * hi2], axis=1)
    _post_norm_residual(y, x_ref, mod_ref, g_ref, o_ref, ga_i)


def _combine(ys, slot1, slot2, info, x, mod, g, row_of, ga_i, row0=0, n_rows=None):
    t, d = x.shape
    n_rows = t if n_rows is None else n_rows
    half = d // 2
    tb = row_of.tile(512)
    nb = t // tb
    b0 = row0 // tb
    smem = functools.partial(pl.BlockSpec, (1, 1, tb), lambda i: (b0 + i, 0, 0), memory_space=pltpu.SMEM)
    return pl.pallas_call(
        functools.partial(_combine_kernel, tb=tb, ga_i=ga_i),
        grid=(n_rows // tb,),
        in_specs=[smem(), smem(),
                  pl.BlockSpec((tb, LANES), lambda i: (b0 + i, 0)),
                  pl.BlockSpec((tb, d), lambda i: (b0 + i, 0)),
                  pl.BlockSpec((None, 6, d), lambda i: (row_of((b0 + i) * tb), 0, 0)),
                  pl.BlockSpec((1, d), lambda i: (0, 0)),
                  pl.BlockSpec(memory_space=pl.ANY)],
        out_specs=pl.BlockSpec((tb, d), lambda i: (i, 0)),
        out_shape=jax.ShapeDtypeStruct((n_rows, d), F32),
        scratch_shapes=[pltpu.VMEM((tb, half), jnp.uint32), pltpu.VMEM((tb, half), jnp.uint32),
                        pltpu.SemaphoreType.DMA(())],
        compiler_params=_params("arbitrary"),
        name="combine",
    )(slot1.reshape(nb, 1, tb), slot2.reshape(nb, 1, tb), info, x, mod, g.reshape(1, d), ys)


def _route_plan(info, counts, n_exp, tm, n_tiles):
    cnt = counts[0, :n_exp].astype(jnp.int32)
    tiles = (cnt + tm - 1) // tm
    ends = jnp.cumsum(tiles)
    base = (ends - tiles) * tm
    i1 = info[:, 0].astype(jnp.int32)
    i2 = info[:, 1].astype(jnp.int32)
    slot1 = base[i1] + info[:, 2].astype(jnp.int32)
    slot2 = base[i2] + info[:, 3].astype(jnp.int32)
    tile_ids = jnp.arange(n_tiles, dtype=jnp.int32)
    tile_expert = jnp.minimum(jnp.sum((ends[None, :] <= tile_ids[:, None]).astype(jnp.int32), axis=1), n_exp - 1)
    return slot1, slot2, tile_expert, ends[-1:].astype(jnp.int32)


def kernel(x_prompt, x_sample, state_gla, c, c_ctx, w_ada, b_ada, g_mix_pre, g_mix_post, w_in, w_alpha_up, b_alpha, g_head, w_gla_br, w_fn_br, w_out, g_ffn_pre, g_ffn_post, w_ffn_gate, w_ffn_up, w_ffn_down, w_router, w_exp_gate, w_exp_up, w_exp_down):
    bp, np_, d = x_prompt.shape
    bs, ns_, _ = x_sample.shape
    depth = w_ada.shape[0]
    heads = GLA_HEADS
    qk = w_alpha_up.shape[-1]
    dk = qk // heads
    vw = g_head.shape[-1]
    dv = vw // heads
    fw = w_fn_br.shape[1]
    gw = fw // FN_GROUPS
    n_exp = w_router.shape[-1]
    p_rows = bp * np_
    t = p_rows + bs * ns_
    assert dk == LANES and 2 * GLA_LR <= LANES and p_rows % ns_ == 0 and ns_ % np_ == 0

    row_of = _Rows(p_rows, ns_)

    col = {"q": 0, "k": qk, "v": 2 * qk, "r": 2 * qk + vw, "f": 2 * qk + 2 * vw,
           "ga": 2 * qk + 2 * vw + fw, "gb": 2 * qk + 2 * vw + fw + d, "alr": 2 * qk + 2 * vw + fw + 2 * d}
    src_alr = 2 * qk + 2 * vw
    n_u = col["alr"] + LANES
    assert col["ga"] % d == 0 and col["gb"] % d == 0 and col["f"] % gw == 0

    n_cond = 1 + bs
    cond = jnp.zeros((16, d), F32).at[0].set(c_ctx).at[1:n_cond].set(c)
    mod_all = _ada(cond, w_ada, b_ada)[:, :n_cond].reshape(depth, n_cond, 6, d)

    chan_c, chan_s = _dft_tables(gw, gw ** -0.5)
    chan_tab = jnp.concatenate([chan_c, chan_s], axis=1).astype(BF16)
    pos_tabs = {}
    for n in (np_, ns_):
        pc, ps = _dft_tables(n, n ** -0.5)
        pos_tabs[n] = (pc.astype(BF16), (-ps).astype(BF16))

    x = jnp.concatenate([x_prompt.reshape(p_rows, d), x_sample.reshape(bs * ns_, d)], axis=0)
    ctx_states = []
    for l in range(depth):
        mod = mod_all[l]
        w_l = w_in[l]
        w_perm = jnp.concatenate(
            [w_l[:, :src_alr], w_l[:, src_alr + 2 * GLA_LR:], w_l[:, src_alr:src_alr + 2 * GLA_LR],
             jnp.zeros((d, LANES - 2 * GLA_LR), F32)], axis=1).astype(BF16)
        u = _in_proj(x, mod, g_mix_pre[l], w_perm, row_of, 0, 1)
        wup = w_alpha_up[l].reshape(2, GLA_LR, heads, dk)
        wal = jnp.zeros((heads, LANES, 2 * dk), F32)
        wal = wal.at[:, :GLA_LR, :dk].set(wup[0].transpose(1, 0, 2))
        wal = wal.at[:, GLA_LR:2 * GLA_LR, dk:].set(wup[1].transpose(1, 0, 2)).astype(BF16)
        bal = b_alpha[l].reshape(2, heads, 1, dk).transpose(1, 2, 0, 3).reshape(heads, 1, 2 * dk)
        gh = g_head[l].reshape(heads, 1, dv)
        og, s_fin = _gla(u, wal, bal, gh, None, l, jnp.zeros((t, vw), BF16), row0=0, n_seq=bp, n_rows=np_,
                         col=col, want_state=True)
        og, _ = _gla(u, wal, bal, gh, state_gla, l, og, row0=p_rows, n_seq=bs, n_rows=ns_, col=col,
                     want_state=False)
        ctx_states.append(s_fin)
        xc, xs = _chan_dft(u, chan_tab, col["f"], fw)
        fr = _pos_dft(*pos_tabs[np_], xc, xs, jnp.zeros((t, fw), BF16), row0=0, n_seq=bp, n_rows=np_)
        fr = _pos_dft(*pos_tabs[ns_], xc, xs, fr, row0=p_rows, n_seq=bs, n_rows=ns_)
        x = _mix_out(og, fr, u, x, mod, g_mix_post[l], w_gla_br[l].astype(BF16), w_fn_br[l].astype(BF16),
                     w_out[l].astype(BF16), row_of, col, 2)
        j = l // 2
        if l % 2 == 0:
            hu = _prenorm(x, mod, g_ffn_pre[l], row_of, 3, 4, pack=True)
            tm = _pick(t, 1024)
            n_tiles = t // tm
            yu = _ffn(hu, jnp.zeros((n_tiles,), jnp.int32), jnp.full((1,), n_tiles, jnp.int32),
                      w_ffn_gate[j][None].astype(BF16), w_ffn_up[j][None].astype(BF16),
                      w_ffn_down[j][None].astype(BF16), tm)
            finish = functools.partial(_post, yu, x, mod, g_ffn_post[l], row_of, 5)
        else:
            info, counts = _route(x, mod, g_ffn_pre[l], w_router[j], row_of, 3, 4)
            tm = _pick(t, 1024)
            n_tiles = (TOP_K * t) // tm + n_exp
            slot1, slot2, tile_expert, n_used = _route_plan(info, counts, n_exp, tm, n_tiles)
            xs_rows = _dispatch(x, mod, g_ffn_pre[l], slot1, slot2, n_tiles * tm, row_of, 3, 4)
            ys_rows = _ffn(xs_rows, tile_expert, n_used, w_exp_gate[j].astype(BF16),
                           w_exp_up[j].astype(BF16), w_exp_down[j].astype(BF16), tm)
            finish = functools.partial(_combine, ys_rows, slot1, slot2, info, x, mod, g_ffn_post[l], row_of, 5)
        if l + 1 < depth:
            x = finish()
        else:
            y_prompt = finish(0, p_rows).reshape(bp, np_, d)
            y_sample = finish(p_rows, t - p_rows).reshape(bs, ns_, d)
    new_state = jnp.stack(ctx_states, axis=1).astype(x_prompt.dtype)
    return (y_prompt, y_sample, new_state)
```

```python
import functools
import math

import jax
import jax.numpy as jnp
from jax import lax
from jax.experimental import pallas as pl
from jax.experimental.pallas import tpu as pltpu

GLA_HEADS = 4
GLA_LR = 16
GLA_TAU = 16.0
GLA_CHUNK = 64
FN_GROUPS = 4
TOP_K = 2
EPS = 1e-6

LANES = 128
SLAB = 4 * GLA_CHUNK
GLA_GROUP = 4
VMEM_LIMIT = 56 * 1024 * 1024
BF16 = jnp.bfloat16
F32 = jnp.float32


def _pick(n, pref, mult=8):
    t = min(n, pref)
    t -= t % mult
    while t > mult and n % t:
        t -= mult
    assert t > 0 and n % t == 0, (n, pref)
    return t


class _Rows:
    def __init__(self, p_rows, seq_rows):
        self.p_rows, self.seq_rows = p_rows, seq_rows

    def __call__(self, start):
        return jnp.where(start < self.p_rows, 0, 1 + (start - self.p_rows) // self.seq_rows)

    def tile(self, pref):
        return _pick(math.gcd(self.p_rows, self.seq_rows), pref)


def _params(*sem):
    return pltpu.CompilerParams(dimension_semantics=sem, vmem_limit_bytes=VMEM_LIMIT)


def _silu(x):
    return x / (1.0 + jnp.exp(-x))


def _sigmoid(x):
    return 1.0 / (1.0 + jnp.exp(-x))


def _dot(a, b):
    return jnp.dot(a, b, preferred_element_type=F32)


def _dot_nt(a, b):
    return lax.dot_general(a, b, (((1,), (1,)), ((), ())), preferred_element_type=F32)


def _dot_tn(a, b):
    return lax.dot_general(a, b, (((0,), (0,)), ((), ())), preferred_element_type=F32)


def _pack_pair(h):
    half = h.shape[1] // 2
    bits = lax.bitcast_convert_type(h.astype(BF16).astype(F32), jnp.uint32)
    return (bits[:, half:] & jnp.uint32(0xFFFF0000)) | (bits[:, :half] >> 16)


def _unpack_pair(u):
    lo = lax.bitcast_convert_type(u << 16, F32)
    hi = lax.bitcast_convert_type(u & jnp.uint32(0xFFFF0000), F32)
    return lo, hi


def _ada_kernel(c_ref, w_ref, b_ref, o_ref):
    s = _silu(c_ref[...]).astype(BF16)
    o_ref[...] = _dot(s, w_ref[...].astype(BF16)) + b_ref[...]


def _ada(cond, w_ada, b_ada):
    depth, d, n6 = w_ada.shape
    rows = cond.shape[0]
    tn = _pick(n6, 1536, LANES)
    return pl.pallas_call(
        _ada_kernel,
        grid=(depth, n6 // tn),
        in_specs=[pl.BlockSpec((rows, d), lambda l, j: (0, 0)),
                  pl.BlockSpec((None, d, tn), lambda l, j: (l, 0, j)),
                  pl.BlockSpec((None, 1, tn), lambda l, j: (l, 0, j))],
        out_specs=pl.BlockSpec((None, rows, tn), lambda l, j: (l, 0, j)),
        out_shape=jax.ShapeDtypeStruct((depth, rows, n6), F32),
        compiler_params=_params("parallel", "parallel"),
        name="ada",
    )(cond, w_ada, b_ada.reshape(depth, 1, n6))


def _modulated_norm(x_ref, mod_ref, g_ref, sh_i, sc_i):
    x = x_ref[...]
    y = x * lax.rsqrt(jnp.mean(x * x, axis=-1, keepdims=True) + EPS) * g_ref[...]
    return y * (1.0 + mod_ref[sc_i:sc_i + 1, :]) + mod_ref[sh_i:sh_i + 1, :]


def _prenorm_kernel(x_ref, mod_ref, g_ref, o_ref, *, sh_i, sc_i, pack):
    h = _modulated_norm(x_ref, mod_ref, g_ref, sh_i, sc_i)
    o_ref[...] = _pack_pair(h) if pack else h.astype(BF16)


def _route_kernel(x_ref, mod_ref, g_ref, wr_ref, info_ref, cnt_ref, carry, *, sh_i, sc_i, n_exp):
    i = pl.program_id(0)

    @pl.when(i == 0)
    def _():
        carry[...] = jnp.zeros_like(carry)

    h = _modulated_norm(x_ref, mod_ref, g_ref, sh_i, sc_i)
    w = wr_ref[...]
    hh = h.astype(BF16)
    hl = (h - hh.astype(F32)).astype(BF16)
    wh = w.astype(BF16)
    wl = (w - wh.astype(F32)).astype(BF16)
    logits = _dot(hh, wh) + _dot(hl, wh) + _dot(hh, wl)
    tm = logits.shape[0]
    lane = lax.broadcasted_iota(jnp.int32, (tm, LANES), 1).astype(F32)
    neg = jnp.float32(-jnp.inf)
    l1 = jnp.where(lane < n_exp, logits, neg)
    m1 = jnp.max(l1, axis=-1, keepdims=True)
    i1 = jnp.min(jnp.where(l1 == m1, lane, float(LANES)), axis=-1, keepdims=True)
    l2 = jnp.where(lane == i1, neg, l1)
    m2 = jnp.max(l2, axis=-1, keepdims=True)
    i2 = jnp.min(jnp.where(l2 == m2, lane, float(LANES)), axis=-1, keepdims=True)
    e = jnp.exp(m2 - m1)
    w1 = 1.0 / (1.0 + e)
    w2 = e / (1.0 + e)
    sel1 = lane == i1
    sel2 = lane == i2
    a = jnp.where(sel1 | sel2, 1.0, 0.0)
    r = lax.broadcasted_iota(jnp.int32, (tm, tm), 0)
    c = lax.broadcasted_iota(jnp.int32, (tm, tm), 1)
    before = jnp.where(c < r, 1.0, 0.0).astype(BF16)
    pos = _dot(before, a.astype(BF16)) + carry[...]
    p1 = jnp.sum(jnp.where(sel1, pos, 0.0), axis=-1, keepdims=True)
    p2 = jnp.sum(jnp.where(sel2, pos, 0.0), axis=-1, keepdims=True)
    total = carry[...] + jnp.sum(a, axis=0, keepdims=True)
    carry[...] = total
    cnt_ref[...] = jnp.broadcast_to(total, cnt_ref.shape)
    info = jnp.where(lane == 0, i1, 0.0)
    info = jnp.where(lane == 1, i2, info)
    info = jnp.where(lane == 2, p1, info)
    info = jnp.where(lane == 3, p2, info)
    info = jnp.where(lane == 4, w1, info)
    info = jnp.where(lane == 5, w2, info)
    info_ref[...] = info


def _mod_spec(d, tm, row_of):
    return pl.BlockSpec((None, 6, d), lambda i: (row_of(i * tm), 0, 0))


def _prenorm(x, mod, g, row_of, sh_i, sc_i, pack):
    t, d = x.shape
    tm = row_of.tile(512)
    out = jax.ShapeDtypeStruct((t, d // 2), jnp.uint32) if pack else jax.ShapeDtypeStruct((t, d), BF16)
    ow = d // 2 if pack else d
    return pl.pallas_call(
        functools.partial(_prenorm_kernel, sh_i=sh_i, sc_i=sc_i, pack=pack),
        grid=(t // tm,),
        in_specs=[pl.BlockSpec((tm, d), lambda i: (i, 0)),
                  _mod_spec(d, tm, row_of),
                  pl.BlockSpec((1, d), lambda i: (0, 0))],
        out_specs=pl.BlockSpec((tm, ow), lambda i: (i, 0)),
        out_shape=out,
        compiler_params=_params("parallel"),
        name="prenorm",
    )(x, mod, g.reshape(1, d))


def _route(x, mod, g, w_router, row_of, sh_i, sc_i):
    t, d = x.shape
    n_exp = w_router.shape[1]
    tm = row_of.tile(512)
    wr = jnp.zeros((d, LANES), F32).at[:, :n_exp].set(w_router)
    return pl.pallas_call(
        functools.partial(_route_kernel, sh_i=sh_i, sc_i=sc_i, n_exp=n_exp),
        grid=(t // tm,),
        in_specs=[pl.BlockSpec((tm, d), lambda i: (i, 0)),
                  _mod_spec(d, tm, row_of),
                  pl.BlockSpec((1, d), lambda i: (0, 0)),
                  pl.BlockSpec((d, LANES), lambda i: (0, 0))],
        out_specs=[pl.BlockSpec((tm, LANES), lambda i: (i, 0)),
                   pl.BlockSpec((8, LANES), lambda i: (0, 0))],
        out_shape=[jax.ShapeDtypeStruct((t, LANES), F32),
                   jax.ShapeDtypeStruct((8, LANES), F32)],
        scratch_shapes=[pltpu.VMEM((1, LANES), F32)],
        compiler_params=_params("arbitrary"),
        name="route",
    )(x, mod, g.reshape(1, d), wr)


def _in_proj_kernel(x_ref, mod_ref, g_ref, w_ref, o_ref, h_scr, *, sh_i, sc_i):
    @pl.when(pl.program_id(1) == 0)
    def _():
        h_scr[...] = _modulated_norm(x_ref, mod_ref, g_ref, sh_i, sc_i).astype(BF16)

    o_ref[...] = _dot(h_scr[...], w_ref[...]).astype(o_ref.dtype)


def _in_proj(x, mod, g, w, layer, row_of, sh_i, sc_i):
    t, d = x.shape
    n = w.shape[2]
    tm = row_of.tile(1024)
    tn = _pick(n, 1664, LANES)
    return pl.pallas_call(
        functools.partial(_in_proj_kernel, sh_i=sh_i, sc_i=sc_i),
        grid=(t // tm, n // tn),
        in_specs=[pl.BlockSpec((tm, d), lambda i, j: (i, 0)),
                  pl.BlockSpec((None, 6, d), lambda i, j: (row_of(i * tm), 0, 0)),
                  pl.BlockSpec((1, d), lambda i, j: (0, 0)),
                  pl.BlockSpec((None, d, tn), lambda i, j: (layer, 0, j))],
        out_specs=pl.BlockSpec((tm, tn), lambda i, j: (i, j)),
        out_shape=jax.ShapeDtypeStruct((t, n), BF16),
        scratch_shapes=[pltpu.VMEM((tm, d), BF16)],
        compiler_params=_params("parallel", "arbitrary"),
        name="in_proj",
    )(x, mod, g.reshape(1, d), w)


def _gla_kernel(*refs, n_rows, seqs, has_s0, has_prev, want_state):
    q_ref, k_ref, v_ref, r_ref, alr_ref, wal_ref, bal_ref, gh_ref = refs[:8]
    pos = 8
    s0_ref = None
    if has_s0:
        s0_ref = refs[pos]
        pos += 1
    if has_prev:
        pos += 1
    og_ref = refs[pos]
    pos += 1
    sfin_ref = None
    if want_state:
        sfin_ref = refs[pos]
        pos += 1
    of_scr, ob_scr, st_scr = refs[pos:pos + 3]
    o_scr = (of_scr, ob_scr)

    dk = q_ref.shape[1]
    n_slab = n_rows // SLAB
    n_chunk = SLAB // GLA_CHUNK
    scale = dk ** -0.5

    ri = lax.broadcasted_iota(jnp.int32, (SLAB, SLAB), 0)
    ci = lax.broadcasted_iota(jnp.int32, (SLAB, SLAB), 1)
    shift = GLA_CHUNK.bit_length() - 1
    same = jnp.right_shift(ri, shift) == jnp.right_shift(ci, shift)
    keep = (same & (ci <= ri), same & (ci >= ri))
    sum_mats = tuple(jnp.where(m, 1.0, 0.0).astype(BF16) for m in keep)

    for s in range(seqs):
        for d in range(2):
            if has_s0:
                st_scr[s, d] = s0_ref[d].T
            else:
                st_scr[s, d] = jnp.zeros(st_scr.shape[2:], F32)

    group = math.gcd(n_slab, GLA_GROUP)
    order = (tuple(range(n_chunk)), tuple(range(n_chunk - 1, -1, -1)))
    chunk_rows = [slice(c * GLA_CHUNK, (c + 1) * GLA_CHUNK) for c in range(n_chunk)]

    def body(i, carry):
        jobs = []
        for s in range(seqs):
            jobs += [(0, pl.multiple_of(s * n_rows + (i * group + g) * SLAB, SLAB), s) for g in range(group)]
            jobs += [(1, pl.multiple_of(s * n_rows + (n_slab - 1 - (i * group + g)) * SLAB, SLAB), s)
                     for g in range(group)]
        cols = [slice(d * dk, (d + 1) * dk) for d, _, _ in jobs]
        rows = [pl.ds(r0, SLAB) for _, r0, _ in jobs]
        z = [_dot(alr_ref[rw, :], wal_ref[:, cl]) + bal_ref[:, cl] for rw, cl in zip(rows, cols)]
        parts = []
        for zz in z:
            la = (jnp.minimum(zz, 0.0) - jnp.log1p(jnp.exp(-jnp.abs(zz)))) * (1.0 / GLA_TAU)
            hi = la.astype(BF16)
            rem = la - hi.astype(F32)
            mid = rem.astype(BF16)
            lo = (rem - mid.astype(F32)).astype(BF16)
            parts.append(jnp.concatenate([hi, mid, lo], axis=1))
        ct = [_dot(sum_mats[d], p) for (d, _, _), p in zip(jobs, parts)]
        qd, ki, ke, dec, v = [], [], [], [], []
        for (d, _, _), rw, c3 in zip(jobs, rows, ct):
            cum = c3[:, :dk] + c3[:, dk:2 * dk] + c3[:, 2 * dk:]
            edge = GLA_CHUNK - 1 if d == 0 else 0
            tot_rows = [cum[c * GLA_CHUNK + edge:c * GLA_CHUNK + edge + 1, :] for c in range(n_chunk)]
            tot = jnp.concatenate([jnp.broadcast_to(r, (GLA_CHUNK, dk)) for r in tot_rows], axis=0)
            q = q_ref[rw, :].astype(F32) * scale
            k = k_ref[rw, :].astype(F32)
            qd.append((q * jnp.exp(cum)).astype(BF16))
            ki.append((k * jnp.exp(-cum)).astype(BF16))
            ke.append((k * jnp.exp(tot - cum)).astype(BF16))
            dec.append([jnp.exp(r) for r in tot_rows])
            v.append(v_ref[rw, :])
        n_jobs = len(jobs)
        scores = [_dot_nt(qd[n], ki[n]) for n in range(n_jobs)]
        kv = [[_dot_tn(v[n][sl], ke[n][sl]) for sl in chunk_rows] for n in range(n_jobs)]
        o = [_dot(jnp.where(keep[jobs[n][0]], scores[n], 0.0).astype(BF16), v[n]) for n in range(n_jobs)]
        st_in = [[None] * n_chunk for _ in range(n_jobs)]
        for s in range(seqs):
            for d in range(2):
                st = st_scr[s, d]
                for n in range(n_jobs):
                    if jobs[n][0] != d or jobs[n][2] != s:
                        continue
                    for c in order[d]:
                        st_in[n][c] = st.astype(BF16)
                        st = dec[n][c] * st + kv[n][c]
                st_scr[s, d] = st
        for n, (d, r0, _) in enumerate(jobs):
            for c in order[d]:
                o_inter = _dot_nt(qd[n][chunk_rows[c]], st_in[n][c])
                o_scr[d][pl.ds(r0 + c * GLA_CHUNK, GLA_CHUNK), :] = o[n][chunk_rows[c]] + o_inter
        return carry

    lax.fori_loop(0, n_slab // group, body, 0)

    def finish(i, carry):
        rows = pl.ds(pl.multiple_of(i * SLAB, SLAB), SLAB)
        o = of_scr[rows, :] + ob_scr[rows, :]
        o = o * lax.rsqrt(jnp.mean(o * o, axis=-1, keepdims=True) + EPS) * gh_ref[...]
        og_ref[rows, :] = (o * _silu(r_ref[rows, :].astype(F32))).astype(og_ref.dtype)
        return carry

    lax.fori_loop(0, seqs * n_slab, finish, 0)
    if want_state:
        for s in range(seqs):
            for d in range(2):
                sfin_ref[s, d] = st_scr[s, d].T


def _gla(u, wal, bal, gh, s0, layer, prev, *, row0, n_seq, n_rows, col, want_state):
    t = u.shape[0]
    heads, dk2 = wal.shape[0], wal.shape[2]
    dk = dk2 // 2
    dv = gh.shape[2]
    has_s0 = s0 is not None
    has_prev = prev is not None
    seqs = 1 if has_s0 else math.gcd(n_seq, GLA_GROUP // math.gcd(n_rows // SLAB, GLA_GROUP))
    br = seqs * n_rows
    assert row0 % br == 0 and n_rows % SLAB == 0
    rb0 = row0 // br

    def at(col_units):
        return lambda b, h: (rb0 + b, col_units + h)

    in_specs = [pl.BlockSpec((br, dk), at(col["q"] // dk)),
                pl.BlockSpec((br, dk), at(col["k"] // dk)),
                pl.BlockSpec((br, dv), at(col["v"] // dv)),
                pl.BlockSpec((br, dv), at(col["r"] // dv)),
                pl.BlockSpec((br, LANES), lambda b, h: (rb0 + b, col["alr"] // LANES)),
                pl.BlockSpec((None, LANES, dk2), lambda b, h: (h, 0, 0)),
                pl.BlockSpec((None, 1, dk2), lambda b, h: (h, 0, 0)),
                pl.BlockSpec((None, 1, dv), lambda b, h: (h, 0, 0))]
    args = [u, u, u, u, u, wal, bal, gh]
    if has_s0:
        in_specs.append(pl.BlockSpec((None, None, 2, None, dk, dv), lambda b, h: (b, layer, 0, h, 0, 0)))
        args.append(s0)
    aliases = {}
    if has_prev:
        aliases = {len(args): 0}
        in_specs.append(pl.BlockSpec(memory_space=pl.ANY))
        args.append(prev)
    out_specs = [pl.BlockSpec((br, dv), lambda b, h: (rb0 + b, h))]
    out_shape = [jax.ShapeDtypeStruct((t, heads * dv), BF16)]
    if want_state:
        out_specs.append(pl.BlockSpec((seqs, 2, None, dk, dv), lambda b, h: (b, 0, h, 0, 0)))
        out_shape.append(jax.ShapeDtypeStruct((n_seq, 2, heads, dk, dv), F32))
    res = pl.pallas_call(
        functools.partial(_gla_kernel, n_rows=n_rows, seqs=seqs, has_s0=has_s0, has_prev=has_prev,
                          want_state=want_state),
        grid=(n_seq // seqs, heads),
        in_specs=in_specs,
        out_specs=out_specs,
        out_shape=out_shape,
        input_output_aliases=aliases,
        scratch_shapes=[pltpu.VMEM((br, dv), F32), pltpu.VMEM((br, dv), F32),
                        pltpu.VMEM((seqs, 2, dv, dk), F32)],
        compiler_params=_params("parallel", "parallel"),
        name="gla",
    )(*args)
    return res if want_state else (res[0], None)


def _chan_dft_kernel(x_ref, tab_ref, xc_ref, xs_ref):
    y = _dot(x_ref[...], tab_ref[...])
    w = xc_ref.shape[1]
    xc_ref[...] = y[:, :w].astype(xc_ref.dtype)
    xs_ref[...] = y[:, w:].astype(xs_ref.dtype)


def _chan_dft(u, tab, col_f, width):
    t = u.shape[0]
    gw = tab.shape[0]
    tm = _pick(t, 2048)
    c0 = col_f // gw
    spec_o = pl.BlockSpec((tm, gw), lambda i, g: (i, g))
    return pl.pallas_call(
        _chan_dft_kernel,
        grid=(t // tm, width // gw),
        in_specs=[pl.BlockSpec((tm, gw), lambda i, g: (i, c0 + g)),
                  pl.BlockSpec((gw, 2 * gw), lambda i, g: (0, 0))],
        out_specs=[spec_o, spec_o],
        out_shape=[jax.ShapeDtypeStruct((t, width), BF16)] * 2,
        compiler_params=_params("parallel", "parallel"),
        name="chan_dft",
    )(u, tab)


def _pos_dft_kernel(tc_ref, ts_ref, xc_ref, xs_ref, *rest):
    o_ref = rest[-1]
    o_ref[...] = (_dot(tc_ref[...], xc_ref[...]) + _dot(ts_ref[...], xs_ref[...])).astype(o_ref.dtype)


def _pos_dft(tc, ts, xc, xs, prev, *, row0, n_seq, n_rows):
    t, width = xc.shape
    assert row0 % n_rows == 0
    rb0 = row0 // n_rows
    to = _pick(n_rows, 512)
    nt = n_rows // to
    once = pl.Buffered(1)
    in_specs = [pl.BlockSpec((to, n_rows), lambda b, i: (i, 0)),
                pl.BlockSpec((to, n_rows), lambda b, i: (i, 0)),
                pl.BlockSpec((n_rows, width), lambda b, i: (rb0 + b, 0), pipeline_mode=once),
                pl.BlockSpec((n_rows, width), lambda b, i: (rb0 + b, 0), pipeline_mode=once)]
    args = [tc, ts, xc, xs]
    aliases = {}
    if prev is not None:
        aliases = {len(args): 0}
        in_specs.append(pl.BlockSpec(memory_space=pl.ANY))
        args.append(prev)
    return pl.pallas_call(
        _pos_dft_kernel,
        grid=(n_seq, nt),
        in_specs=in_specs,
        out_specs=pl.BlockSpec((to, width), lambda b, i: ((rb0 + b) * nt + i, 0)),
        out_shape=jax.ShapeDtypeStruct((t, width), BF16),
        input_output_aliases=aliases,
        compiler_params=_params("parallel", "parallel"),
        name="pos_dft",
    )(*args)


def _dft_tables(n, scale, split=64):
    k = jnp.arange(n, dtype=jnp.int32)

    def direct(j):
        ang = ((j[:, None] * k[None, :]) % n).astype(F32) * (2.0 * math.pi / n)
        return jnp.cos(ang), jnp.sin(ang)

    if n <= 4 * split or n % split:
        c, s = direct(k)
    else:
        ca, sa = direct(jnp.arange(0, n, split, dtype=jnp.int32))
        cb, sb = direct(jnp.arange(split, dtype=jnp.int32))
        c = (ca[:, None, :] * cb[None] - sa[:, None, :] * sb[None]).reshape(n, n)
        s = (sa[:, None, :] * cb[None] + ca[:, None, :] * sb[None]).reshape(n, n)
    return c * scale, s * scale


def _mix_out_kernel(og_ref, fr_ref, ga_ref, gb_ref, x_ref, mod_ref, g_ref, wa_ref, wb_ref, wo_ref,
                    o_ref, *, ga_i):
    ba = _dot(og_ref[...], wa_ref[...])
    bb = _dot(fr_ref[...], wb_ref[...])
    merged = _sigmoid(ga_ref[...].astype(F32)) * ba + _sigmoid(gb_ref[...].astype(F32)) * bb
    out = _dot(merged.astype(BF16), wo_ref[...])
    y = out * lax.rsqrt(jnp.mean(out * out, axis=-1, keepdims=True) + EPS) * g_ref[...]
    o_ref[...] = x_ref[...] + mod_ref[ga_i:ga_i + 1, :] * y


def _mix_out(og, fr, u, x, mod, g, wa, wb, wo, layer, row_of, col, ga_i):
    t, d = x.shape
    w = og.shape[1]
    tm = row_of.tile(512)
    once = pl.Buffered(1)
    return pl.pallas_call(
        functools.partial(_mix_out_kernel, ga_i=ga_i),
        grid=(t // tm,),
        in_specs=[pl.BlockSpec((tm, w), lambda i: (i, 0)),
                  pl.BlockSpec((tm, w), lambda i: (i, 0)),
                  pl.BlockSpec((tm, d), lambda i: (i, col["ga"] // d)),
                  pl.BlockSpec((tm, d), lambda i: (i, col["gb"] // d)),
                  pl.BlockSpec((tm, d), lambda i: (i, 0)),
                  _mod_spec(d, tm, row_of),
                  pl.BlockSpec((1, d), lambda i: (0, 0)),
                  pl.BlockSpec((None, w, d), lambda i: (layer, 0, 0), pipeline_mode=once),
                  pl.BlockSpec((None, w, d), lambda i: (layer, 0, 0), pipeline_mode=once),
                  pl.BlockSpec((None, d, d), lambda i: (layer, 0, 0), pipeline_mode=once)],
        out_specs=pl.BlockSpec((tm, d), lambda i: (i, 0)),
        out_shape=jax.ShapeDtypeStruct((t, d), F32),
        compiler_params=_params("parallel"),
        name="mix_out",
    )(og, fr, u, u, x, mod, g.reshape(1, d), wa, wb, wo)


def _ffn_kernel(te_ref, nu_ref, x_ref, wg_ref, wu_ref, wd_ref, o_ref, xb, acc):
    g = pl.program_id(0)
    j = pl.program_id(1)
    nj = pl.num_programs(1)
    half = x_ref.shape[1]
    used = g < nu_ref[0]

    @pl.when(used & (j == 0))
    def _():
        lo, hi = _unpack_pair(x_ref[...])
        xb[:, :half] = lo.astype(BF16)
        xb[:, half:] = hi.astype(BF16)
        acc[...] = jnp.zeros_like(acc)

    @pl.when(used)
    def _():
        x = xb[...]
        a = _silu(_dot(x, wg_ref[...])) * _dot(x, wu_ref[...])
        acc[...] += _dot(a.astype(BF16), wd_ref[...])

    @pl.when(j == nj - 1)
    def _():
        @pl.when(used)
        def _():
            o_ref[...] = _pack_pair(acc[...])

        @pl.when(jnp.logical_not(used))
        def _():
            o_ref[...] = jnp.zeros_like(o_ref)


def _ffn(xu, tile_expert, n_used, wg, wu, wd, layer, tm):
    s, half = xu.shape
    _, _, d, f = wg.shape
    tf = _pick(f, 512, LANES)
    nj = f // tf
    n_tiles = s // tm

    def jj(g, j, nu):
        return jnp.where(g < nu[0], j, nj - 1)

    grid_spec = pltpu.PrefetchScalarGridSpec(
        num_scalar_prefetch=2,
        grid=(n_tiles, nj),
        in_specs=[pl.BlockSpec((tm, half), lambda g, j, te, nu: (g, 0)),
                  pl.BlockSpec((None, None, d, tf), lambda g, j, te, nu: (layer, te[g], 0, jj(g, j, nu))),
                  pl.BlockSpec((None, None, d, tf), lambda g, j, te, nu: (layer, te[g], 0, jj(g, j, nu))),
                  pl.BlockSpec((None, None, tf, d), lambda g, j, te, nu: (layer, te[g], jj(g, j, nu), 0))],
        out_specs=pl.BlockSpec((tm, half), lambda g, j, te, nu: (g, 0)),
        scratch_shapes=[pltpu.VMEM((tm, d), BF16), pltpu.VMEM((tm, d), F32)])
    return pl.pallas_call(
        _ffn_kernel,
        grid_spec=grid_spec,
        out_shape=jax.ShapeDtypeStruct((s, half), jnp.uint32),
        compiler_params=_params("parallel", "arbitrary"),
        name="ffn",
    )(tile_expert, n_used, xu, wg, wu, wd)


def _post_norm_residual(y, x_ref, mod_ref, g_ref, o_ref, ga_i):
    yn = y * lax.rsqrt(jnp.mean(y * y, axis=-1, keepdims=True) + EPS) * g_ref[...]
    o_ref[...] = x_ref[...] + mod_ref[ga_i:ga_i + 1, :] * yn


def _post_kernel(y_ref, x_ref, mod_ref, g_ref, o_ref, *, ga_i):
    lo, hi = _unpack_pair(y_ref[...])
    _post_norm_residual(jnp.concatenate([lo, hi], axis=1), x_ref, mod_ref, g_ref, o_ref, ga_i)


def _post(yu, x, mod, g, row_of, ga_i, row0=0, n_rows=None):
    t, d = x.shape
    n_rows = t if n_rows is None else n_rows
    tm = row_of.tile(512)
    b0 = row0 // tm
    return pl.pallas_call(
        functools.partial(_post_kernel, ga_i=ga_i),
        grid=(n_rows // tm,),
        in_specs=[pl.BlockSpec((tm, d // 2), lambda i: (b0 + i, 0)),
                  pl.BlockSpec((tm, d), lambda i: (b0 + i, 0)),
                  pl.BlockSpec((None, 6, d), lambda i: (row_of((b0 + i) * tm), 0, 0)),
                  pl.BlockSpec((1, d), lambda i: (0, 0))],
        out_specs=pl.BlockSpec((tm, d), lambda i: (i, 0)),
        out_shape=jax.ShapeDtypeStruct((n_rows, d), F32),
        compiler_params=_params("parallel"),
        name="post",
    )(yu, x, mod, g.reshape(1, d))


def _row_copy(src_ref, src_row, dst_ref, dst_row, sem):
    return pltpu.make_async_copy(src_ref.at[pl.ds(src_row, 1), :], dst_ref.at[pl.ds(dst_row, 1), :], sem)


def _dispatch_kernel(s1_ref, s2_ref, x_ref, mod_ref, g_ref, init_ref, xs_ref, buf, sem, *, tb, sh_i, sc_i):
    del init_ref
    i = pl.program_id(0)
    n = pl.num_programs(0)
    cur = i % 2

    def drain(b):
        whole = pltpu.make_async_copy(buf.at[b], xs_ref.at[pl.ds(0, tb), :], sem.at[b])
        whole.wait()
        whole.wait()

    @pl.when(i >= 2)
    def _():
        drain(cur)

    buf[cur] = _pack_pair(_modulated_norm(x_ref, mod_ref, g_ref, sh_i, sc_i))

    def issue(t, carry):
        _row_copy(buf.at[cur], t, xs_ref, s1_ref[0, 0, t], sem.at[cur]).start()
        _row_copy(buf.at[cur], t, xs_ref, s2_ref[0, 0, t], sem.at[cur]).start()
        return carry

    lax.fori_loop(0, tb, issue, 0, unroll=8)

    @pl.when(i == n - 1)
    def _():
        drain(cur)

        @pl.when(n >= 2)
        def _():
            drain(1 - cur)


def _dispatch(x, mod, g, slot1, slot2, n_slots, row_of, sh_i, sc_i):
    t, d = x.shape
    half = d // 2
    tb = row_of.tile(512)
    nb = t // tb
    smem = functools.partial(pl.BlockSpec, (1, 1, tb), lambda i: (i, 0, 0), memory_space=pltpu.SMEM)
    return pl.pallas_call(
        functools.partial(_dispatch_kernel, tb=tb, sh_i=sh_i, sc_i=sc_i),
        grid=(nb,),
        in_specs=[smem(), smem(),
                  pl.BlockSpec((tb, d), lambda i: (i, 0)),
                  _mod_spec(d, tb, row_of),
                  pl.BlockSpec((1, d), lambda i: (0, 0)),
                  pl.BlockSpec(memory_space=pl.ANY)],
        out_specs=pl.BlockSpec(memory_space=pl.ANY),
        out_shape=jax.ShapeDtypeStruct((n_slots, half), jnp.uint32),
        scratch_shapes=[pltpu.VMEM((2, tb, half), jnp.uint32), pltpu.SemaphoreType.DMA((2,))],
        input_output_aliases={5: 0},
        compiler_params=_params("arbitrary"),
        name="dispatch",
    )(slot1.reshape(nb, 1, tb), slot2.reshape(nb, 1, tb), x, mod, g.reshape(1, d),
      jnp.zeros((n_slots, half), jnp.uint32))


def _combine_kernel(s1_ref, s2_ref, info_ref, x_ref, mod_ref, g_ref, ys_ref, o_ref, b1, b2, sem, *,
                    tb, ga_i):
    def issue(t, carry):
        _row_copy(ys_ref, s1_ref[0, 0, t], b1, t, sem).start()
        _row_copy(ys_ref, s2_ref[0, 0, t], b2, t, sem).start()
        return carry

    lax.fori_loop(0, tb, issue, 0, unroll=8)

    pltpu.make_async_copy(ys_ref.at[pl.ds(0, tb), :], b1, sem).wait()
    pltpu.make_async_copy(ys_ref.at[pl.ds(0, tb), :], b2, sem).wait()
    info = info_ref[...]
    w1 = info[:, 4:5]
    w2 = info[:, 5:6]
    lo1, hi1 = _unpack_pair(b1[...])
    lo2, hi2 = _unpack_pair(b2[...])
    y = jnp.concatenate([w1 * lo1 + w2 * lo2, w1 * hi1 + w2 * hi2], axis=1)
    _post_norm_residual(y, x_ref, mod_ref, g_ref, o_ref, ga_i)


def _combine(ys, slot1, slot2, info, x, mod, g, row_of, ga_i, row0=0, n_rows=None):
    t, d = x.shape
    n_rows = t if n_rows is None else n_rows
    half = d // 2
    tb = row_of.tile(512)
    nb = t // tb
    b0 = row0 // tb
    smem = functools.partial(pl.BlockSpec, (1, 1, tb), lambda i: (b0 + i, 0, 0), memory_space=pltpu.SMEM)
    return pl.pallas_call(
        functools.partial(_combine_kernel, tb=tb, ga_i=ga_i),
        grid=(n_rows // tb,),
        in_specs=[smem(), smem(),
                  pl.BlockSpec((tb, LANES), lambda i: (b0 + i, 0)),
                  pl.BlockSpec((tb, d), lambda i: (b0 + i, 0)),
                  pl.BlockSpec((None, 6, d), lambda i: (row_of((b0 + i) * tb), 0, 0)),
                  pl.BlockSpec((1, d), lambda i: (0, 0)),
                  pl.BlockSpec(memory_space=pl.ANY)],
        out_specs=pl.BlockSpec((tb, d), lambda i: (i, 0)),
        out_shape=jax.ShapeDtypeStruct((n_rows, d), F32),
        scratch_shapes=[pltpu.VMEM((tb, half), jnp.uint32), pltpu.VMEM((tb, half), jnp.uint32),
                        pltpu.SemaphoreType.DMA(())],
        compiler_params=_params("arbitrary"),
        name="combine",
    )(slot1.reshape(nb, 1, tb), slot2.reshape(nb, 1, tb), info, x, mod, g.reshape(1, d), ys)


def _route_plan(info, counts, n_exp, tm, n_tiles):
    cnt = counts[0, :n_exp].astype(jnp.int32)
    tiles = (cnt + tm - 1) // tm
    ends = jnp.cumsum(tiles)
    base = (ends - tiles) * tm
    i1 = info[:, 0].astype(jnp.int32)
    i2 = info[:, 1].astype(jnp.int32)
    slot1 = base[i1] + info[:, 2].astype(jnp.int32)
    slot2 = base[i2] + info[:, 3].astype(jnp.int32)
    tile_ids = jnp.arange(n_tiles, dtype=jnp.int32)
    tile_expert = jnp.minimum(jnp.sum((ends[None, :] <= tile_ids[:, None]).astype(jnp.int32), axis=1), n_exp - 1)
    return slot1, slot2, tile_expert, ends[-1:].astype(jnp.int32)


def kernel(x_prompt, x_sample, state_gla, c, c_ctx, w_ada, b_ada, g_mix_pre, g_mix_post, w_in, w_alpha_up, b_alpha, g_head, w_gla_br, w_fn_br, w_out, g_ffn_pre, g_ffn_post, w_ffn_gate, w_ffn_up, w_ffn_down, w_router, w_exp_gate, w_exp_up, w_exp_down):
    bp, np_, d = x_prompt.shape
    bs, ns_, _ = x_sample.shape
    depth = w_ada.shape[0]
    heads = GLA_HEADS
    qk = w_alpha_up.shape[-1]
    dk = qk // heads
    vw = g_head.shape[-1]
    dv = vw // heads
    fw = w_fn_br.shape[1]
    gw = fw // FN_GROUPS
    n_exp = w_router.shape[-1]
    p_rows = bp * np_
    t = p_rows + bs * ns_
    assert dk == LANES and 2 * GLA_LR <= LANES and p_rows % ns_ == 0 and ns_ % np_ == 0

    row_of = _Rows(p_rows, ns_)

    col = {"q": 0, "k": qk, "v": 2 * qk, "r": 2 * qk + vw, "f": 2 * qk + 2 * vw,
           "ga": 2 * qk + 2 * vw + fw, "gb": 2 * qk + 2 * vw + fw + d, "alr": 2 * qk + 2 * vw + fw + 2 * d}
    src_alr = 2 * qk + 2 * vw
    n_u = col["alr"] + LANES
    assert col["ga"] % d == 0 and col["gb"] % d == 0 and col["f"] % gw == 0

    n_cond = 1 + bs
    cond = jnp.zeros((16, d), F32).at[0].set(c_ctx).at[1:n_cond].set(c)
    mod_all = _ada(cond, w_ada, b_ada)[:, :n_cond].reshape(depth, n_cond, 6, d)

    chan_c, chan_s = _dft_tables(gw, gw ** -0.5)
    chan_tab = jnp.concatenate([chan_c, chan_s], axis=1).astype(BF16)
    pos_tabs = {}
    for n in (np_, ns_):
        pc, ps = _dft_tables(n, n ** -0.5)
        pos_tabs[n] = (pc.astype(BF16), (-ps).astype(BF16))

    x = jnp.concatenate([x_prompt.reshape(p_rows, d), x_sample.reshape(bs * ns_, d)], axis=0)
    w_perm = jnp.concatenate(
        [w_in[:, :, :src_alr], w_in[:, :, src_alr + 2 * GLA_LR:], w_in[:, :, src_alr:src_alr + 2 * GLA_LR],
         jnp.zeros((depth, d, LANES - 2 * GLA_LR), F32)], axis=2).astype(BF16)
    wa_all, wb_all, wo_all = w_gla_br.astype(BF16), w_fn_br.astype(BF16), w_out.astype(BF16)
    dense_w = [w[:, None].astype(BF16) for w in (w_ffn_gate, w_ffn_up, w_ffn_down)]
    exp_w = [w.astype(BF16) for w in (w_exp_gate, w_exp_up, w_exp_down)]
    ctx_states = []
    for l in range(depth):
        mod = mod_all[l]
        u = _in_proj(x, mod, g_mix_pre[l], w_perm, l, row_of, 0, 1)
        wup = w_alpha_up[l].reshape(2, GLA_LR, heads, dk)
        wal = jnp.zeros((heads, LANES, 2 * dk), F32)
        wal = wal.at[:, :GLA_LR, :dk].set(wup[0].transpose(1, 0, 2))
        wal = wal.at[:, GLA_LR:2 * GLA_LR, dk:].set(wup[1].transpose(1, 0, 2)).astype(BF16)
        bal = b_alpha[l].reshape(2, heads, 1, dk).transpose(1, 2, 0, 3).reshape(heads, 1, 2 * dk)
        gh = g_head[l].reshape(heads, 1, dv)
        og, s_fin = _gla(u, wal, bal, gh, None, l, jnp.zeros((t, vw), BF16), row0=0, n_seq=bp, n_rows=np_,
                         col=col, want_state=True)
        og, _ = _gla(u, wal, bal, gh, state_gla, l, og, row0=p_rows, n_seq=bs, n_rows=ns_, col=col,
                     want_state=False)
        ctx_states.append(s_fin)
        xc, xs = _chan_dft(u, chan_tab, col["f"], fw)
        fr = _pos_dft(*pos_tabs[np_], xc, xs, jnp.zeros((t, fw), BF16), row0=0, n_seq=bp, n_rows=np_)
        fr = _pos_dft(*pos_tabs[ns_], xc, xs, fr, row0=p_rows, n_seq=bs, n_rows=ns_)
        x = _mix_out(og, fr, u, x, mod, g_mix_post[l], wa_all, wb_all, wo_all, l, row_of, col, 2)
        j = l // 2
        if l % 2 == 0:
            hu = _prenorm(x, mod, g_ffn_pre[l], row_of, 3, 4, pack=True)
            tm = _pick(t, 1024)
            n_tiles = t // tm
            yu = _ffn(hu, jnp.zeros((n_tiles,), jnp.int32), jnp.full((1,), n_tiles, jnp.int32), *dense_w, j, tm)
            finish = functools.partial(_post, yu, x, mod, g_ffn_post[l], row_of, 5)
        else:
            info, counts = _route(x, mod, g_ffn_pre[l], w_router[j], row_of, 3, 4)
            tm = _pick(t, 1024)
            n_tiles = (TOP_K * t) // tm + n_exp
            slot1, slot2, tile_expert, n_used = _route_plan(info, counts, n_exp, tm, n_tiles)
            xs_rows = _dispatch(x, mod, g_ffn_pre[l], slot1, slot2, n_tiles * tm, row_of, 3, 4)
            ys_rows = _ffn(xs_rows, tile_expert, n_used, *exp_w, j, tm)
            finish = functools.partial(_combine, ys_rows, slot1, slot2, info, x, mod, g_ffn_post[l], row_of, 5)
        if l + 1 < depth:
            x = finish()
        else:
            y_prompt = finish(0, p_rows).reshape(bp, np_, d)
            y_sample = finish(p_rows, t - p_rows).reshape(bs, ns_, d)
    new_state = jnp.stack(ctx_states, axis=1).astype(x_prompt.dtype)
    return (y_prompt, y_sample, new_state)
```

```python
import functools
import math

import jax
import jax.numpy as jnp
from jax import lax
from jax.experimental import pallas as pl
from jax.experimental.pallas import tpu as pltpu

GLA_HEADS = 4
GLA_LR = 16
GLA_TAU = 16.0
GLA_CHUNK = 64
FN_GROUPS = 4
TOP_K = 2
EPS = 1e-6

LANES = 128
SLAB = 4 * GLA_CHUNK
GLA_GROUP = 4
VMEM_LIMIT = 56 * 1024 * 1024
BF16 = jnp.bfloat16
F32 = jnp.float32


def _pick(n, pref, mult=8):
    t = min(n, pref)
    t -= t % mult
    while t > mult and n % t:
        t -= mult
    assert t > 0 and n % t == 0, (n, pref)
    return t


class _Rows:
    def __init__(self, p_rows, seq_rows):
        self.p_rows, self.seq_rows = p_rows, seq_rows

    def __call__(self, start):
        return jnp.where(start < self.p_rows, 0, 1 + (start - self.p_rows) // self.seq_rows)

    def tile(self, pref):
        return _pick(math.gcd(self.p_rows, self.seq_rows), pref)


def _params(*sem):
    return pltpu.CompilerParams(dimension_semantics=sem, vmem_limit_bytes=VMEM_LIMIT)


def _silu(x):
    return x / (1.0 + jnp.exp(-x))


def _sigmoid(x):
    return 1.0 / (1.0 + jnp.exp(-x))


def _dot(a, b):
    return jnp.dot(a, b, preferred_element_type=F32)


def _dot_nt(a, b):
    return lax.dot_general(a, b, (((1,), (1,)), ((), ())), preferred_element_type=F32)


def _dot_tn(a, b):
    return lax.dot_general(a, b, (((0,), (0,)), ((), ())), preferred_element_type=F32)


def _pack_pair(h):
    half = h.shape[1] // 2
    bits = lax.bitcast_convert_type(h.astype(BF16).astype(F32), jnp.uint32)
    return (bits[:, half:] & jnp.uint32(0xFFFF0000)) | (bits[:, :half] >> 16)


def _unpack_pair(u):
    lo = lax.bitcast_convert_type(u << 16, F32)
    hi = lax.bitcast_convert_type(u & jnp.uint32(0xFFFF0000), F32)
    return lo, hi


def _ada_kernel(c_ref, w_ref, b_ref, o_ref):
    s = _silu(c_ref[...]).astype(BF16)
    o_ref[...] = _dot(s, w_ref[...].astype(BF16)) + b_ref[...]


def _ada(cond, w_ada, b_ada):
    depth, d, n6 = w_ada.shape
    rows = cond.shape[0]
    tn = _pick(n6, 1536, LANES)
    return pl.pallas_call(
        _ada_kernel,
        grid=(depth, n6 // tn),
        in_specs=[pl.BlockSpec((rows, d), lambda l, j: (0, 0)),
                  pl.BlockSpec((None, d, tn), lambda l, j: (l, 0, j)),
                  pl.BlockSpec((None, 1, tn), lambda l, j: (l, 0, j))],
        out_specs=pl.BlockSpec((None, rows, tn), lambda l, j: (l, 0, j)),
        out_shape=jax.ShapeDtypeStruct((depth, rows, n6), F32),
        compiler_params=_params("parallel", "parallel"),
        name="ada",
    )(cond, w_ada, b_ada.reshape(depth, 1, n6))


def _modulated_norm(x_ref, mod_ref, g_ref, sh_i, sc_i):
    x = x_ref[...]
    y = x * lax.rsqrt(jnp.mean(x * x, axis=-1, keepdims=True) + EPS) * g_ref[...]
    return y * (1.0 + mod_ref[sc_i:sc_i + 1, :]) + mod_ref[sh_i:sh_i + 1, :]


def _prenorm_kernel(x_ref, mod_ref, g_ref, o_ref, *, sh_i, sc_i, pack):
    h = _modulated_norm(x_ref, mod_ref, g_ref, sh_i, sc_i)
    o_ref[...] = _pack_pair(h) if pack else h.astype(BF16)


def _route_kernel(x_ref, mod_ref, g_ref, wr_ref, info_ref, cnt_ref, carry, *, sh_i, sc_i, n_exp):
    i = pl.program_id(0)

    @pl.when(i == 0)
    def _():
        carry[...] = jnp.zeros_like(carry)

    h = _modulated_norm(x_ref, mod_ref, g_ref, sh_i, sc_i)
    w = wr_ref[...]
    hh = h.astype(BF16)
    hl = (h - hh.astype(F32)).astype(BF16)
    wh = w.astype(BF16)
    wl = (w - wh.astype(F32)).astype(BF16)
    logits = _dot(hh, wh) + _dot(hl, wh) + _dot(hh, wl)
    tm = logits.shape[0]
    lane = lax.broadcasted_iota(jnp.int32, (tm, LANES), 1).astype(F32)
    neg = jnp.float32(-jnp.inf)
    l1 = jnp.where(lane < n_exp, logits, neg)
    m1 = jnp.max(l1, axis=-1, keepdims=True)
    i1 = jnp.min(jnp.where(l1 == m1, lane, float(LANES)), axis=-1, keepdims=True)
    l2 = jnp.where(lane == i1, neg, l1)
    m2 = jnp.max(l2, axis=-1, keepdims=True)
    i2 = jnp.min(jnp.where(l2 == m2, lane, float(LANES)), axis=-1, keepdims=True)
    e = jnp.exp(m2 - m1)
    w1 = 1.0 / (1.0 + e)
    w2 = e / (1.0 + e)
    sel1 = lane == i1
    sel2 = lane == i2
    a = jnp.where(sel1 | sel2, 1.0, 0.0)
    r = lax.broadcasted_iota(jnp.int32, (tm, tm), 0)
    c = lax.broadcasted_iota(jnp.int32, (tm, tm), 1)
    before = jnp.where(c < r, 1.0, 0.0).astype(BF16)
    pos = _dot(before, a.astype(BF16)) + carry[...]
    p1 = jnp.sum(jnp.where(sel1, pos, 0.0), axis=-1, keepdims=True)
    p2 = jnp.sum(jnp.where(sel2, pos, 0.0), axis=-1, keepdims=True)
    total = carry[...] + jnp.sum(a, axis=0, keepdims=True)
    carry[...] = total
    cnt_ref[...] = jnp.broadcast_to(total, cnt_ref.shape)
    info = jnp.where(lane == 0, i1, 0.0)
    info = jnp.where(lane == 1, i2, info)
    info = jnp.where(lane == 2, p1, info)
    info = jnp.where(lane == 3, p2, info)
    info = jnp.where(lane == 4, w1, info)
    info = jnp.where(lane == 5, w2, info)
    info_ref[...] = info


def _mod_spec(d, tm, row_of):
    return pl.BlockSpec((None, 6, d), lambda i: (row_of(i * tm), 0, 0))


def _prenorm(x, mod, g, row_of, sh_i, sc_i, pack):
    t, d = x.shape
    tm = row_of.tile(512)
    out = jax.ShapeDtypeStruct((t, d // 2), jnp.uint32) if pack else jax.ShapeDtypeStruct((t, d), BF16)
    ow = d // 2 if pack else d
    return pl.pallas_call(
        functools.partial(_prenorm_kernel, sh_i=sh_i, sc_i=sc_i, pack=pack),
        grid=(t // tm,),
        in_specs=[pl.BlockSpec((tm, d), lambda i: (i, 0)),
                  _mod_spec(d, tm, row_of),
                  pl.BlockSpec((1, d), lambda i: (0, 0))],
        out_specs=pl.BlockSpec((tm, ow), lambda i: (i, 0)),
        out_shape=out,
        compiler_params=_params("parallel"),
        name="prenorm",
    )(x, mod, g.reshape(1, d))


def _route(x, mod, g, w_router, row_of, sh_i, sc_i):
    t, d = x.shape
    n_exp = w_router.shape[1]
    tm = row_of.tile(512)
    wr = jnp.zeros((d, LANES), F32).at[:, :n_exp].set(w_router)
    return pl.pallas_call(
        functools.partial(_route_kernel, sh_i=sh_i, sc_i=sc_i, n_exp=n_exp),
        grid=(t // tm,),
        in_specs=[pl.BlockSpec((tm, d), lambda i: (i, 0)),
                  _mod_spec(d, tm, row_of),
                  pl.BlockSpec((1, d), lambda i: (0, 0)),
                  pl.BlockSpec((d, LANES), lambda i: (0, 0))],
        out_specs=[pl.BlockSpec((tm, LANES), lambda i: (i, 0)),
                   pl.BlockSpec((8, LANES), lambda i: (0, 0))],
        out_shape=[jax.ShapeDtypeStruct((t, LANES), F32),
                   jax.ShapeDtypeStruct((8, LANES), F32)],
        scratch_shapes=[pltpu.VMEM((1, LANES), F32)],
        compiler_params=_params("arbitrary"),
        name="route",
    )(x, mod, g.reshape(1, d), wr)


def _in_proj_kernel(x_ref, mod_ref, g_ref, w_ref, o_ref, h_scr, *, sh_i, sc_i):
    @pl.when(pl.program_id(1) == 0)
    def _():
        h_scr[...] = _modulated_norm(x_ref, mod_ref, g_ref, sh_i, sc_i).astype(BF16)

    o_ref[...] = _dot(h_scr[...], w_ref[...]).astype(o_ref.dtype)


def _in_proj(x, mod, g, w, layer, row_of, sh_i, sc_i):
    t, d = x.shape
    n = w.shape[2]
    tm = row_of.tile(1024)
    tn = _pick(n, 1664, LANES)
    return pl.pallas_call(
        functools.partial(_in_proj_kernel, sh_i=sh_i, sc_i=sc_i),
        grid=(t // tm, n // tn),
        in_specs=[pl.BlockSpec((tm, d), lambda i, j: (i, 0)),
                  pl.BlockSpec((None, 6, d), lambda i, j: (row_of(i * tm), 0, 0)),
                  pl.BlockSpec((1, d), lambda i, j: (0, 0)),
                  pl.BlockSpec((None, d, tn), lambda i, j: (layer, 0, j))],
        out_specs=pl.BlockSpec((tm, tn), lambda i, j: (i, j)),
        out_shape=jax.ShapeDtypeStruct((t, n), BF16),
        scratch_shapes=[pltpu.VMEM((tm, d), BF16)],
        compiler_params=_params("parallel", "arbitrary"),
        name="in_proj",
    )(x, mod, g.reshape(1, d), w)


def _gla_kernel(*refs, n_rows, seqs, has_s0, has_prev, want_state):
    q_ref, k_ref, v_ref, r_ref, alr_ref, wal_ref, bal_ref, gh_ref = refs[:8]
    pos = 8
    s0_ref = None
    if has_s0:
        s0_ref = refs[pos]
        pos += 1
    if has_prev:
        pos += 1
    og_ref = refs[pos]
    pos += 1
    sfin_ref = None
    if want_state:
        sfin_ref = refs[pos]
        pos += 1
    of_scr, ob_scr, st_scr = refs[pos:pos + 3]
    o_scr = (of_scr, ob_scr)

    dk = q_ref.shape[1]
    n_slab = n_rows // SLAB
    n_chunk = SLAB // GLA_CHUNK
    scale = dk ** -0.5

    ri = lax.broadcasted_iota(jnp.int32, (SLAB, SLAB), 0)
    ci = lax.broadcasted_iota(jnp.int32, (SLAB, SLAB), 1)
    shift = GLA_CHUNK.bit_length() - 1
    same = jnp.right_shift(ri, shift) == jnp.right_shift(ci, shift)
    keep = (same & (ci <= ri), same & (ci >= ri))
    sum_mats = tuple(jnp.where(m, 1.0, 0.0).astype(BF16) for m in keep)

    for s in range(seqs):
        for d in range(2):
            if has_s0:
                st_scr[s, d] = s0_ref[d].T
            else:
                st_scr[s, d] = jnp.zeros(st_scr.shape[2:], F32)

    group = math.gcd(n_slab, GLA_GROUP)
    order = (tuple(range(n_chunk)), tuple(range(n_chunk - 1, -1, -1)))
    chunk_rows = [slice(c * GLA_CHUNK, (c + 1) * GLA_CHUNK) for c in range(n_chunk)]

    def body(i, carry):
        jobs = []
        for s in range(seqs):
            jobs += [(0, pl.multiple_of(s * n_rows + (i * group + g) * SLAB, SLAB), s) for g in range(group)]
            jobs += [(1, pl.multiple_of(s * n_rows + (n_slab - 1 - (i * group + g)) * SLAB, SLAB), s)
                     for g in range(group)]
        cols = [slice(d * dk, (d + 1) * dk) for d, _, _ in jobs]
        rows = [pl.ds(r0, SLAB) for _, r0, _ in jobs]
        z = [_dot(alr_ref[rw, :], wal_ref[:, cl]) + bal_ref[:, cl] for rw, cl in zip(rows, cols)]
        parts = []
        for zz in z:
            la = (jnp.minimum(zz, 0.0) - jnp.log1p(jnp.exp(-jnp.abs(zz)))) * (1.0 / GLA_TAU)
            hi = la.astype(BF16)
            rem = la - hi.astype(F32)
            mid = rem.astype(BF16)
            lo = (rem - mid.astype(F32)).astype(BF16)
            parts.append(jnp.concatenate([hi, mid, lo], axis=1))
        ct = [_dot(sum_mats[d], p) for (d, _, _), p in zip(jobs, parts)]
        qd, ki, ke, dec, v = [], [], [], [], []
        for (d, _, _), rw, c3 in zip(jobs, rows, ct):
            cum = c3[:, :dk] + c3[:, dk:2 * dk] + c3[:, 2 * dk:]
            edge = GLA_CHUNK - 1 if d == 0 else 0
            tot_rows = [cum[c * GLA_CHUNK + edge:c * GLA_CHUNK + edge + 1, :] for c in range(n_chunk)]
            tot = jnp.concatenate([jnp.broadcast_to(r, (GLA_CHUNK, dk)) for r in tot_rows], axis=0)
            q = q_ref[rw, :].astype(F32) * scale
            k = k_ref[rw, :].astype(F32)
            qd.append((q * jnp.exp(cum)).astype(BF16))
            ki.append((k * jnp.exp(-cum)).astype(BF16))
            ke.append((k * jnp.exp(tot - cum)).astype(BF16))
            dec.append([jnp.exp(r) for r in tot_rows])
            v.append(v_ref[rw, :])
        n_jobs = len(jobs)
        scores = [_dot_nt(qd[n], ki[n]) for n in range(n_jobs)]
        kv = [[_dot_tn(v[n][sl], ke[n][sl]) for sl in chunk_rows] for n in range(n_jobs)]
        o = [_dot(jnp.where(keep[jobs[n][0]], scores[n], 0.0).astype(BF16), v[n]) for n in range(n_jobs)]
        st_in = [[None] * n_chunk for _ in range(n_jobs)]
        for s in range(seqs):
            for d in range(2):
                st = st_scr[s, d]
                for n in range(n_jobs):
                    if jobs[n][0] != d or jobs[n][2] != s:
                        continue
                    for c in order[d]:
                        st_in[n][c] = st.astype(BF16)
                        st = dec[n][c] * st + kv[n][c]
                st_scr[s, d] = st
        for n, (d, r0, _) in enumerate(jobs):
            for c in order[d]:
                o_inter = _dot_nt(qd[n][chunk_rows[c]], st_in[n][c])
                o_scr[d][pl.ds(r0 + c * GLA_CHUNK, GLA_CHUNK), :] = o[n][chunk_rows[c]] + o_inter
        return carry

    lax.fori_loop(0, n_slab // group, body, 0)

    def finish(i, carry):
        rows = pl.ds(pl.multiple_of(i * SLAB, SLAB), SLAB)
        o = of_scr[rows, :] + ob_scr[rows, :]
        o = o * lax.rsqrt(jnp.mean(o * o, axis=-1, keepdims=True) + EPS) * gh_ref[...]
        og_ref[rows, :] = (o * _silu(r_ref[rows, :].astype(F32))).astype(og_ref.dtype)
        return carry

    lax.fori_loop(0, seqs * n_slab, finish, 0)
    if want_state:
        for s in range(seqs):
            for d in range(2):
                sfin_ref[s, d] = st_scr[s, d].T


def _gla(u, wal, bal, gh, s0, layer, prev, *, row0, n_seq, n_rows, col, want_state):
    t = u.shape[0]
    heads, dk2 = wal.shape[0], wal.shape[2]
    dk = dk2 // 2
    dv = gh.shape[2]
    has_s0 = s0 is not None
    has_prev = prev is not None
    seqs = 1 if has_s0 else math.gcd(n_seq, GLA_GROUP // math.gcd(n_rows // SLAB, GLA_GROUP))
    br = seqs * n_rows
    assert row0 % br == 0 and n_rows % SLAB == 0
    rb0 = row0 // br

    def at(col_units):
        return lambda b, h: (rb0 + b, col_units + h)

    in_specs = [pl.BlockSpec((br, dk), at(col["q"] // dk)),
                pl.BlockSpec((br, dk), at(col["k"] // dk)),
                pl.BlockSpec((br, dv), at(col["v"] // dv)),
                pl.BlockSpec((br, dv), at(col["r"] // dv)),
                pl.BlockSpec((br, LANES), lambda b, h: (rb0 + b, col["alr"] // LANES)),
                pl.BlockSpec((None, LANES, dk2), lambda b, h: (h, 0, 0)),
                pl.BlockSpec((None, 1, dk2), lambda b, h: (h, 0, 0)),
                pl.BlockSpec((None, 1, dv), lambda b, h: (h, 0, 0))]
    args = [u, u, u, u, u, wal, bal, gh]
    if has_s0:
        in_specs.append(pl.BlockSpec((None, None, 2, None, dk, dv), lambda b, h: (b, layer, 0, h, 0, 0)))
        args.append(s0)
    aliases = {}
    if has_prev:
        aliases = {len(args): 0}
        in_specs.append(pl.BlockSpec(memory_space=pl.ANY))
        args.append(prev)
    out_specs = [pl.BlockSpec((br, dv), lambda b, h: (rb0 + b, h))]
    out_shape = [jax.ShapeDtypeStruct((t, heads * dv), BF16)]
    if want_state:
        out_specs.append(pl.BlockSpec((seqs, 2, None, dk, dv), lambda b, h: (b, 0, h, 0, 0)))
        out_shape.append(jax.ShapeDtypeStruct((n_seq, 2, heads, dk, dv), F32))
    res = pl.pallas_call(
        functools.partial(_gla_kernel, n_rows=n_rows, seqs=seqs, has_s0=has_s0, has_prev=has_prev,
                          want_state=want_state),
        grid=(n_seq // seqs, heads),
        in_specs=in_specs,
        out_specs=out_specs,
        out_shape=out_shape,
        input_output_aliases=aliases,
        scratch_shapes=[pltpu.VMEM((br, dv), F32), pltpu.VMEM((br, dv), F32),
                        pltpu.VMEM((seqs, 2, dv, dk), F32)],
        compiler_params=_params("parallel", "parallel"),
        name="gla",
    )(*args)
    return res if want_state else (res[0], None)


def _chan_dft_kernel(x_ref, tab_ref, xc_ref, xs_ref):
    y = _dot(x_ref[...], tab_ref[...])
    w = xc_ref.shape[1]
    xc_ref[...] = y[:, :w].astype(xc_ref.dtype)
    xs_ref[...] = y[:, w:].astype(xs_ref.dtype)


def _chan_dft(u, tab, col_f, width, row0, n_rows):
    gw = tab.shape[0]
    tm = _pick(math.gcd(row0, n_rows) if row0 else n_rows, 2048)
    c0 = col_f // gw
    i0 = row0 // tm
    spec_o = pl.BlockSpec((tm, gw), lambda i, g: (i, g))
    return pl.pallas_call(
        _chan_dft_kernel,
        grid=(n_rows // tm, width // gw),
        in_specs=[pl.BlockSpec((tm, gw), lambda i, g: (i0 + i, c0 + g)),
                  pl.BlockSpec((gw, 2 * gw), lambda i, g: (0, 0))],
        out_specs=[spec_o, spec_o],
        out_shape=[jax.ShapeDtypeStruct((n_rows, width), BF16)] * 2,
        compiler_params=_params("parallel", "parallel"),
        name="chan_dft",
    )(u, tab)


def _chan_fold_kernel(xm_ref, xa_ref, xt_ref, xh_ref, tab_ref, ap_ref, bm_ref, mid_ref):
    m = pl.program_id(1)
    blk, width = xm_ref.shape
    gw = tab_ref.shape[0]
    r = lax.broadcasted_iota(jnp.int32, (blk, blk), 0)
    c = lax.broadcasted_iota(jnp.int32, (blk, blk), 1)
    flip = jnp.where(r + c == blk, 1.0, 0.0).astype(BF16)
    rev = _dot(flip, xa_ref[...])
    first = jnp.where(m > 0, xt_ref[0:1, :].astype(F32), 0.0)
    rev = jnp.where(lax.broadcasted_iota(jnp.int32, (blk, width), 0) == 0, first, rev)
    xm = xm_ref[...].astype(F32)
    plus = (xm + rev).astype(BF16)
    minus = (xm - rev).astype(BF16)
    xh = xh_ref[...]
    tab = tab_ref[...]
    for g in range(width // gw):
        cs = slice(g * gw, (g + 1) * gw)
        ap_ref[:, cs] = _dot(plus[:, cs], tab[:, :gw]).astype(ap_ref.dtype)
        bm_ref[:, cs] = _dot(minus[:, cs], tab[:, gw:]).astype(bm_ref.dtype)
        mid_ref[:, cs] = _dot(xh[:, cs], tab[:, :gw])


def _chan_fold(u, tab, col_f, width, row0, n_seq, n_rows):
    gw = tab.shape[0]
    blk = SLAB
    sub = 16
    nb = n_rows // blk
    nbh = nb // 2
    assert row0 % blk == 0 and n_rows % (2 * blk) == 0
    base = row0 // blk
    assert col_f % width == 0
    c0 = col_f // width

    def first_of(b, m):
        return base + b * nb + jnp.where(m == 0, 0, nb - m)

    half_rows = n_seq * n_rows // 2
    spec_o = pl.BlockSpec((blk, width), lambda b, m: (b * nbh + m, 0))
    return pl.pallas_call(
        _chan_fold_kernel,
        grid=(n_seq, nbh),
        in_specs=[pl.BlockSpec((blk, width), lambda b, m: (base + b * nb + m, c0)),
                  pl.BlockSpec((blk, width), lambda b, m: (base + b * nb + nb - 1 - m, c0)),
                  pl.BlockSpec((sub, width), lambda b, m: (first_of(b, m) * (blk // sub), c0)),
                  pl.BlockSpec((sub, width), lambda b, m: ((base + b * nb + nbh) * (blk // sub), c0)),
                  pl.BlockSpec((gw, 2 * gw), lambda b, m: (0, 0))],
        out_specs=[spec_o, spec_o, pl.BlockSpec((None, sub, width), lambda b, m: (b, 0, 0))],
        out_shape=[jax.ShapeDtypeStruct((half_rows, width), BF16), jax.ShapeDtypeStruct((half_rows, width), BF16),
                   jax.ShapeDtypeStruct((n_seq, sub, width), F32)],
        compiler_params=_params("parallel", "arbitrary"),
        name="chan_fold",
    )(u, u, u, u, tab)


def _pos_dft_kernel(*refs, fold):
    tc_ref, ts_ref, xc_ref, xs_ref = refs[:4]
    o_ref = refs[-1]
    y = _dot(tc_ref[...], xc_ref[...]) + _dot(ts_ref[...], xs_ref[...])
    if fold:
        sign_ref, mid_ref = refs[4:6]
        y = y + sign_ref[...] * mid_ref[0:1, :]
    o_ref[...] = y.astype(o_ref.dtype)


def _pos_dft(tc, ts, xc, xs, prev, *, row0, n_seq, n_rows, sign=None, mid=None):
    t, width = prev.shape
    k_rows = tc.shape[1]
    fold = sign is not None
    assert row0 % n_rows == 0
    rb0 = row0 // n_rows
    to = _pick(n_rows, 512)
    nt = n_rows // to
    once = pl.Buffered(1)
    in_specs = [pl.BlockSpec((to, k_rows), lambda b, i: (i, 0)),
                pl.BlockSpec((to, k_rows), lambda b, i: (i, 0)),
                pl.BlockSpec((k_rows, width), lambda b, i: (b, 0), pipeline_mode=once),
                pl.BlockSpec((k_rows, width), lambda b, i: (b, 0), pipeline_mode=once)]
    args = [tc, ts, xc, xs]
    if fold:
        in_specs += [pl.BlockSpec((to, 1), lambda b, i: (i, 0)),
                     pl.BlockSpec((None,) + mid.shape[1:], lambda b, i: (b, 0, 0))]
        args += [sign, mid]
    in_specs.append(pl.BlockSpec(memory_space=pl.ANY))
    args.append(prev)
    return pl.pallas_call(
        functools.partial(_pos_dft_kernel, fold=fold),
        grid=(n_seq, nt),
        in_specs=in_specs,
        out_specs=pl.BlockSpec((to, width), lambda b, i: ((rb0 + b) * nt + i, 0)),
        out_shape=jax.ShapeDtypeStruct((t, width), BF16),
        input_output_aliases={len(args) - 1: 0},
        compiler_params=_params("parallel", "parallel"),
        name="pos_dft",
    )(*args)


def _dft_tables(n, scale, split=64):
    k = jnp.arange(n, dtype=jnp.int32)

    def direct(j):
        ang = ((j[:, None] * k[None, :]) % n).astype(F32) * (2.0 * math.pi / n)
        return jnp.cos(ang), jnp.sin(ang)

    if n <= 4 * split or n % split:
        c, s = direct(k)
    else:
        ca, sa = direct(jnp.arange(0, n, split, dtype=jnp.int32))
        cb, sb = direct(jnp.arange(split, dtype=jnp.int32))
        c = (ca[:, None, :] * cb[None] - sa[:, None, :] * sb[None]).reshape(n, n)
        s = (sa[:, None, :] * cb[None] + ca[:, None, :] * sb[None]).reshape(n, n)
    return c * scale, s * scale


def _mix_out_kernel(og_ref, fr_ref, ga_ref, gb_ref, x_ref, mod_ref, g_ref, wa_ref, wb_ref, wo_ref,
                    o_ref, *, ga_i):
    ba = _dot(og_ref[...], wa_ref[...])
    bb = _dot(fr_ref[...], wb_ref[...])
    merged = _sigmoid(ga_ref[...].astype(F32)) * ba + _sigmoid(gb_ref[...].astype(F32)) * bb
    out = _dot(merged.astype(BF16), wo_ref[...])
    y = out * lax.rsqrt(jnp.mean(out * out, axis=-1, keepdims=True) + EPS) * g_ref[...]
    o_ref[...] = x_ref[...] + mod_ref[ga_i:ga_i + 1, :] * y


def _mix_out(og, fr, u, x, mod, g, wa, wb, wo, layer, row_of, col, ga_i):
    t, d = x.shape
    w = og.shape[1]
    tm = row_of.tile(512)
    once = pl.Buffered(1)
    return pl.pallas_call(
        functools.partial(_mix_out_kernel, ga_i=ga_i),
        grid=(t // tm,),
        in_specs=[pl.BlockSpec((tm, w), lambda i: (i, 0)),
                  pl.BlockSpec((tm, w), lambda i: (i, 0)),
                  pl.BlockSpec((tm, d), lambda i: (i, col["ga"] // d)),
                  pl.BlockSpec((tm, d), lambda i: (i, col["gb"] // d)),
                  pl.BlockSpec((tm, d), lambda i: (i, 0)),
                  _mod_spec(d, tm, row_of),
                  pl.BlockSpec((1, d), lambda i: (0, 0)),
                  pl.BlockSpec((None, w, d), lambda i: (layer, 0, 0), pipeline_mode=once),
                  pl.BlockSpec((None, w, d), lambda i: (layer, 0, 0), pipeline_mode=once),
                  pl.BlockSpec((None, d, d), lambda i: (layer, 0, 0), pipeline_mode=once)],
        out_specs=pl.BlockSpec((tm, d), lambda i: (i, 0)),
        out_shape=jax.ShapeDtypeStruct((t, d), F32),
        compiler_params=_params("parallel"),
        name="mix_out",
    )(og, fr, u, u, x, mod, g.reshape(1, d), wa, wb, wo)


def _ffn_kernel(te_ref, nu_ref, x_ref, wg_ref, wu_ref, wd_ref, o_ref, xb, acc):
    g = pl.program_id(0)
    j = pl.program_id(1)
    nj = pl.num_programs(1)
    half = x_ref.shape[1]
    used = g < nu_ref[0]

    @pl.when(used & (j == 0))
    def _():
        lo, hi = _unpack_pair(x_ref[...])
        xb[:, :half] = lo.astype(BF16)
        xb[:, half:] = hi.astype(BF16)
        acc[...] = jnp.zeros_like(acc)

    @pl.when(used)
    def _():
        x = xb[...]
        a = _silu(_dot(x, wg_ref[...])) * _dot(x, wu_ref[...])
        acc[...] += _dot(a.astype(BF16), wd_ref[...])

    @pl.when(j == nj - 1)
    def _():
        @pl.when(used)
        def _():
            o_ref[...] = _pack_pair(acc[...])

        @pl.when(jnp.logical_not(used))
        def _():
            o_ref[...] = jnp.zeros_like(o_ref)


def _ffn(xu, tile_expert, n_used, wg, wu, wd, layer, tm):
    s, half = xu.shape
    _, _, d, f = wg.shape
    tf = _pick(f, 512, LANES)
    nj = f // tf
    n_tiles = s // tm

    def jj(g, j, nu):
        return jnp.where(g < nu[0], j, nj - 1)

    grid_spec = pltpu.PrefetchScalarGridSpec(
        num_scalar_prefetch=2,
        grid=(n_tiles, nj),
        in_specs=[pl.BlockSpec((tm, half), lambda g, j, te, nu: (g, 0)),
                  pl.BlockSpec((None, None, d, tf), lambda g, j, te, nu: (layer, te[g], 0, jj(g, j, nu))),
                  pl.BlockSpec((None, None, d, tf), lambda g, j, te, nu: (layer, te[g], 0, jj(g, j, nu))),
                  pl.BlockSpec((None, None, tf, d), lambda g, j, te, nu: (layer, te[g], jj(g, j, nu), 0))],
        out_specs=pl.BlockSpec((tm, half), lambda g, j, te, nu: (g, 0)),
        scratch_shapes=[pltpu.VMEM((tm, d), BF16), pltpu.VMEM((tm, d), F32)])
    return pl.pallas_call(
        _ffn_kernel,
        grid_spec=grid_spec,
        out_shape=jax.ShapeDtypeStruct((s, half), jnp.uint32),
        compiler_params=_params("parallel", "arbitrary"),
        name="ffn",
    )(tile_expert, n_used, xu, wg, wu, wd)


def _post_norm_residual(y, x_ref, mod_ref, g_ref, o_ref, ga_i):
    yn = y * lax.rsqrt(jnp.mean(y * y, axis=-1, keepdims=True) + EPS) * g_ref[...]
    o_ref[...] = x_ref[...] + mod_ref[ga_i:ga_i + 1, :] * yn


def _post_kernel(y_ref, x_ref, mod_ref, g_ref, o_ref, *, ga_i):
    lo, hi = _unpack_pair(y_ref[...])
    _post_norm_residual(jnp.concatenate([lo, hi], axis=1), x_ref, mod_ref, g_ref, o_ref, ga_i)


def _post(yu, x, mod, g, row_of, ga_i, row0=0, n_rows=None):
    t, d = x.shape
    n_rows = t if n_rows is None else n_rows
    tm = row_of.tile(512)
    b0 = row0 // tm
    return pl.pallas_call(
        functools.partial(_post_kernel, ga_i=ga_i),
        grid=(n_rows // tm,),
        in_specs=[pl.BlockSpec((tm, d // 2), lambda i: (b0 + i, 0)),
                  pl.BlockSpec((tm, d), lambda i: (b0 + i, 0)),
                  pl.BlockSpec((None, 6, d), lambda i: (row_of((b0 + i) * tm), 0, 0)),
                  pl.BlockSpec((1, d), lambda i: (0, 0))],
        out_specs=pl.BlockSpec((tm, d), lambda i: (i, 0)),
        out_shape=jax.ShapeDtypeStruct((n_rows, d), F32),
        compiler_params=_params("parallel"),
        name="post",
    )(yu, x, mod, g.reshape(1, d))


def _row_copy(src_ref, src_row, dst_ref, dst_row, sem):
    return pltpu.make_async_copy(src_ref.at[pl.ds(src_row, 1), :], dst_ref.at[pl.ds(dst_row, 1), :], sem)


def _dispatch_kernel(s1_ref, s2_ref, x_ref, mod_ref, g_ref, init_ref, xs_ref, buf, sem, *, tb, sh_i, sc_i):
    del init_ref
    i = pl.program_id(0)
    n = pl.num_programs(0)
    cur = i % 2

    def drain(b):
        whole = pltpu.make_async_copy(buf.at[b], xs_ref.at[pl.ds(0, tb), :], sem.at[b])
        whole.wait()
        whole.wait()

    @pl.when(i >= 2)
    def _():
        drain(cur)

    buf[cur] = _pack_pair(_modulated_norm(x_ref, mod_ref, g_ref, sh_i, sc_i))

    def issue(t, carry):
        _row_copy(buf.at[cur], t, xs_ref, s1_ref[0, 0, t], sem.at[cur]).start()
        _row_copy(buf.at[cur], t, xs_ref, s2_ref[0, 0, t], sem.at[cur]).start()
        return carry

    lax.fori_loop(0, tb, issue, 0, unroll=8)

    @pl.when(i == n - 1)
    def _():
        drain(cur)

        @pl.when(n >= 2)
        def _():
            drain(1 - cur)


def _dispatch(x, mod, g, slot1, slot2, n_slots, row_of, sh_i, sc_i):
    t, d = x.shape
    half = d // 2
    tb = row_of.tile(512)
    nb = t // tb
    smem = functools.partial(pl.BlockSpec, (1, 1, tb), lambda i: (i, 0, 0), memory_space=pltpu.SMEM)
    return pl.pallas_call(
        functools.partial(_dispatch_kernel, tb=tb, sh_i=sh_i, sc_i=sc_i),
        grid=(nb,),
        in_specs=[smem(), smem(),
                  pl.BlockSpec((tb, d), lambda i: (i, 0)),
                  _mod_spec(d, tb, row_of),
                  pl.BlockSpec((1, d), lambda i: (0, 0)),
                  pl.BlockSpec(memory_space=pl.ANY)],
        out_specs=pl.BlockSpec(memory_space=pl.ANY),
        out_shape=jax.ShapeDtypeStruct((n_slots, half), jnp.uint32),
        scratch_shapes=[pltpu.VMEM((2, tb, half), jnp.uint32), pltpu.SemaphoreType.DMA((2,))],
        input_output_aliases={5: 0},
        compiler_params=_params("arbitrary"),
        name="dispatch",
    )(slot1.reshape(nb, 1, tb), slot2.reshape(nb, 1, tb), x, mod, g.reshape(1, d),
      jnp.zeros((n_slots, half), jnp.uint32))


def _combine_kernel(s1_ref, s2_ref, info_ref, x_ref, mod_ref, g_ref, ys_ref, o_ref, b1, b2, sem, *,
                    tb, ga_i):
    def issue(t, carry):
        _row_copy(ys_ref, s1_ref[0, 0, t], b1, t, sem).start()
        _row_copy(ys_ref, s2_ref[0, 0, t], b2, t, sem).start()
        return carry

    lax.fori_loop(0, tb, issue, 0, unroll=8)

    pltpu.make_async_copy(ys_ref.at[pl.ds(0, tb), :], b1, sem).wait()
    pltpu.make_async_copy(ys_ref.at[pl.ds(0, tb), :], b2, sem).wait()
    info = info_ref[...]
    w1 = info[:, 4:5]
    w2 = info[:, 5:6]
    lo1, hi1 = _unpack_pair(b1[...])
    lo2, hi2 = _unpack_pair(b2[...])
    y = jnp.concatenate([w1 * lo1 + w2 * lo2, w1 * hi1 + w2 * hi2], axis=1)
    _post_norm_residual(y, x_ref, mod_ref, g_ref, o_ref, ga_i)


def _combine(ys, slot1, slot2, info, x, mod, g, row_of, ga_i, row0=0, n_rows=None):
    t, d = x.shape
    n_rows = t if n_rows is None else n_rows
    half = d // 2
    tb = row_of.tile(512)
    nb = t // tb
    b0 = row0 // tb
    smem = functools.partial(pl.BlockSpec, (1, 1, tb), lambda i: (b0 + i, 0, 0), memory_space=pltpu.SMEM)
    return pl.pallas_call(
        functools.partial(_combine_kernel, tb=tb, ga_i=ga_i),
        grid=(n_rows // tb,),
        in_specs=[smem(), smem(),
                  pl.BlockSpec((tb, LANES), lambda i: (b0 + i, 0)),
                  pl.BlockSpec((tb, d), lambda i: (b0 + i, 0)),
                  pl.BlockSpec((None, 6, d), lambda i: (row_of((b0 + i) * tb), 0, 0)),
                  pl.BlockSpec((1, d), lambda i: (0, 0)),
                  pl.BlockSpec(memory_space=pl.ANY)],
        out_specs=pl.BlockSpec((tb, d), lambda i: (i, 0)),
        out_shape=jax.ShapeDtypeStruct((n_rows, d), F32),
        scratch_shapes=[pltpu.VMEM((tb, half), jnp.uint32), pltpu.VMEM((tb, half), jnp.uint32),
                        pltpu.SemaphoreType.DMA(())],
        compiler_params=_params("arbitrary"),
        name="combine",
    )(slot1.reshape(nb, 1, tb), slot2.reshape(nb, 1, tb), info, x, mod, g.reshape(1, d), ys)


def _route_plan(info, counts, n_exp, tm, n_tiles):
    cnt = counts[0, :n_exp].astype(jnp.int32)
    tiles = (cnt + tm - 1) // tm
    ends = jnp.cumsum(tiles)
    base = (ends - tiles) * tm
    i1 = info[:, 0].astype(jnp.int32)
    i2 = info[:, 1].astype(jnp.int32)
    slot1 = base[i1] + info[:, 2].astype(jnp.int32)
    slot2 = base[i2] + info[:, 3].astype(jnp.int32)
    tile_ids = jnp.arange(n_tiles, dtype=jnp.int32)
    tile_expert = jnp.minimum(jnp.sum((ends[None, :] <= tile_ids[:, None]).astype(jnp.int32), axis=1), n_exp - 1)
    return slot1, slot2, tile_expert, ends[-1:].astype(jnp.int32)


def kernel(x_prompt, x_sample, state_gla, c, c_ctx, w_ada, b_ada, g_mix_pre, g_mix_post, w_in, w_alpha_up, b_alpha, g_head, w_gla_br, w_fn_br, w_out, g_ffn_pre, g_ffn_post, w_ffn_gate, w_ffn_up, w_ffn_down, w_router, w_exp_gate, w_exp_up, w_exp_down):
    bp, np_, d = x_prompt.shape
    bs, ns_, _ = x_sample.shape
    depth = w_ada.shape[0]
    heads = GLA_HEADS
    qk = w_alpha_up.shape[-1]
    dk = qk // heads
    vw = g_head.shape[-1]
    dv = vw // heads
    fw = w_fn_br.shape[1]
    gw = fw // FN_GROUPS
    n_exp = w_router.shape[-1]
    p_rows = bp * np_
    t = p_rows + bs * ns_
    assert dk == LANES and 2 * GLA_LR <= LANES and p_rows % ns_ == 0 and ns_ % np_ == 0

    row_of = _Rows(p_rows, ns_)

    col = {"q": 0, "k": qk, "v": 2 * qk, "r": 2 * qk + vw, "f": 2 * qk + 2 * vw,
           "ga": 2 * qk + 2 * vw + fw, "gb": 2 * qk + 2 * vw + fw + d, "alr": 2 * qk + 2 * vw + fw + 2 * d}
    src_alr = 2 * qk + 2 * vw
    n_u = col["alr"] + LANES
    assert col["ga"] % d == 0 and col["gb"] % d == 0 and col["f"] % gw == 0

    n_cond = 1 + bs
    cond = jnp.zeros((16, d), F32).at[0].set(c_ctx).at[1:n_cond].set(c)
    mod_all = _ada(cond, w_ada, b_ada)[:, :n_cond].reshape(depth, n_cond, 6, d)

    chan_c, chan_s = _dft_tables(gw, gw ** -0.5)
    chan_tab = jnp.concatenate([chan_c, chan_s], axis=1).astype(BF16)
    pos_tabs = {}
    for n in (np_, ns_):
        pc, ps = _dft_tables(n, n ** -0.5)
        kk = n if n % (2 * SLAB) else n // 2
        sign = jnp.where(jnp.arange(n) % 2 == 0, 1.0, -1.0).astype(F32).reshape(n, 1) * n ** -0.5
        pos_tabs[n] = (pc[:, :kk].astype(BF16), (-ps[:, :kk]).astype(BF16), sign)

    def fourier(u, fr, row0, n_seq, n):
        tc, ts, sign = pos_tabs[n]
        if tc.shape[1] == n:
            xc, xs = _chan_dft(u, chan_tab, col["f"], fw, row0, n_seq * n)
            return _pos_dft(tc, ts, xc, xs, fr, row0=row0, n_seq=n_seq, n_rows=n)
        ap, bm, mid = _chan_fold(u, chan_tab, col["f"], fw, row0, n_seq, n)
        return _pos_dft(tc, ts, ap, bm, fr, row0=row0, n_seq=n_seq, n_rows=n, sign=sign, mid=mid)

    x = jnp.concatenate([x_prompt.reshape(p_rows, d), x_sample.reshape(bs * ns_, d)], axis=0)
    w_perm = jnp.concatenate(
        [w_in[:, :, :src_alr], w_in[:, :, src_alr + 2 * GLA_LR:], w_in[:, :, src_alr:src_alr + 2 * GLA_LR],
         jnp.zeros((depth, d, LANES - 2 * GLA_LR), F32)], axis=2).astype(BF16)
    wa_all, wb_all, wo_all = w_gla_br.astype(BF16), w_fn_br.astype(BF16), w_out.astype(BF16)
    dense_w = [w[:, None].astype(BF16) for w in (w_ffn_gate, w_ffn_up, w_ffn_down)]
    exp_w = [w.astype(BF16) for w in (w_exp_gate, w_exp_up, w_exp_down)]
    ctx_states = []
    for l in range(depth):
        mod = mod_all[l]
        u = _in_proj(x, mod, g_mix_pre[l], w_perm, l, row_of, 0, 1)
        wup = w_alpha_up[l].reshape(2, GLA_LR, heads, dk)
        wal = jnp.zeros((heads, LANES, 2 * dk), F32)
        wal = wal.at[:, :GLA_LR, :dk].set(wup[0].transpose(1, 0, 2))
        wal = wal.at[:, GLA_LR:2 * GLA_LR, dk:].set(wup[1].transpose(1, 0, 2)).astype(BF16)
        bal = b_alpha[l].reshape(2, heads, 1, dk).transpose(1, 2, 0, 3).reshape(heads, 1, 2 * dk)
        gh = g_head[l].reshape(heads, 1, dv)
        og, s_fin = _gla(u, wal, bal, gh, None, l, jnp.zeros((t, vw), BF16), row0=0, n_seq=bp, n_rows=np_,
                         col=col, want_state=True)
        og, _ = _gla(u, wal, bal, gh, state_gla, l, og, row0=p_rows, n_seq=bs, n_rows=ns_, col=col,
                     want_state=False)
        ctx_states.append(s_fin)
        fr = fourier(u, jnp.zeros((t, fw), BF16), 0, bp, np_)
        fr = fourier(u, fr, p_rows, bs, ns_)
        x = _mix_out(og, fr, u, x, mod, g_mix_post[l], wa_all, wb_all, wo_all, l, row_of, col, 2)
        j = l // 2
        if l % 2 == 0:
            hu = _prenorm(x, mod, g_ffn_pre[l], row_of, 3, 4, pack=True)
            tm = _pick(t, 1024)
            n_tiles = t // tm
            yu = _ffn(hu, jnp.zeros((n_tiles,), jnp.int32), jnp.full((1,), n_tiles, jnp.int32), *dense_w, j, tm)
            finish = functools.partial(_post, yu, x, mod, g_ffn_post[l], row_of, 5)
        else:
            info, counts = _route(x, mod, g_ffn_pre[l], w_router[j], row_of, 3, 4)
            tm = _pick(t, 1024)
            n_tiles = (TOP_K * t) // tm + n_exp
            slot1, slot2, tile_expert, n_used = _route_plan(info, counts, n_exp, tm, n_tiles)
            xs_rows = _dispatch(x, mod, g_ffn_pre[l], slot1, slot2, n_tiles * tm, row_of, 3, 4)
            ys_rows = _ffn(xs_rows, tile_expert, n_used, *exp_w, j, tm)
            finish = functools.partial(_combine, ys_rows, slot1, slot2, info, x, mod, g_ffn_post[l], row_of, 5)
        if l + 1 < depth:
            x = finish()
        else:
            y_prompt = finish(0, p_rows).reshape(bp, np_, d)
            y_sample = finish(p_rows, t - p_rows).reshape(bs, ns_, d)
    new_state = jnp.stack(ctx_states, axis=1).astype(x_prompt.dtype)
    return (y_prompt, y_sample, new_state)
```

```python
import functools
import math

import jax
import jax.numpy as jnp
from jax import lax
from jax.experimental import pallas as pl
from jax.experimental.pallas import tpu as pltpu

GLA_HEADS = 4
GLA_LR = 16
GLA_TAU = 16.0
GLA_CHUNK = 64
FN_GROUPS = 4
TOP_K = 2
EPS = 1e-6

LANES = 128
SLAB = 4 * GLA_CHUNK
GLA_GROUP = 4
VMEM_LIMIT = 56 * 1024 * 1024
BF16 = jnp.bfloat16
F32 = jnp.float32


def _pick(n, pref, mult=8):
    t = min(n, pref)
    t -= t % mult
    while t > mult and n % t:
        t -= mult
    assert t > 0 and n % t == 0, (n, pref)
    return t


class _Rows:
    def __init__(self, p_rows, seq_rows):
        self.p_rows, self.seq_rows = p_rows, seq_rows

    def __call__(self, start):
        return jnp.where(start < self.p_rows, 0, 1 + (start - self.p_rows) // self.seq_rows)

    def tile(self, pref):
        return _pick(math.gcd(self.p_rows, self.seq_rows), pref)


def _params(*sem):
    return pltpu.CompilerParams(dimension_semantics=sem, vmem_limit_bytes=VMEM_LIMIT)


def _silu(x):
    return x / (1.0 + jnp.exp(-x))


def _sigmoid(x):
    return 1.0 / (1.0 + jnp.exp(-x))


def _dot(a, b):
    return jnp.dot(a, b, preferred_element_type=F32)


def _dot_nt(a, b):
    return lax.dot_general(a, b, (((1,), (1,)), ((), ())), preferred_element_type=F32)


def _dot_tn(a, b):
    return lax.dot_general(a, b, (((0,), (0,)), ((), ())), preferred_element_type=F32)


def _pack_pair(h):
    half = h.shape[1] // 2
    bits = lax.bitcast_convert_type(h.astype(BF16).astype(F32), jnp.uint32)
    return (bits[:, half:] & jnp.uint32(0xFFFF0000)) | (bits[:, :half] >> 16)


def _unpack_pair(u):
    lo = lax.bitcast_convert_type(u << 16, F32)
    hi = lax.bitcast_convert_type(u & jnp.uint32(0xFFFF0000), F32)
    return lo, hi


def _ada_kernel(c_ref, w_ref, b_ref, o_ref):
    s = _silu(c_ref[...]).astype(BF16)
    o_ref[...] = _dot(s, w_ref[...].astype(BF16)) + b_ref[...]


def _ada(cond, w_ada, b_ada):
    depth, d, n6 = w_ada.shape
    rows = cond.shape[0]
    tn = _pick(n6, 1536, LANES)
    return pl.pallas_call(
        _ada_kernel,
        grid=(depth, n6 // tn),
        in_specs=[pl.BlockSpec((rows, d), lambda l, j: (0, 0)),
                  pl.BlockSpec((None, d, tn), lambda l, j: (l, 0, j)),
                  pl.BlockSpec((None, 1, tn), lambda l, j: (l, 0, j))],
        out_specs=pl.BlockSpec((None, rows, tn), lambda l, j: (l, 0, j)),
        out_shape=jax.ShapeDtypeStruct((depth, rows, n6), F32),
        compiler_params=_params("parallel", "parallel"),
        name="ada",
    )(cond, w_ada, b_ada.reshape(depth, 1, n6))


def _modulated_norm(x_ref, mod_ref, g_ref, sh_i, sc_i):
    x = x_ref[...]
    y = x * lax.rsqrt(jnp.mean(x * x, axis=-1, keepdims=True) + EPS) * g_ref[...]
    return y * (1.0 + mod_ref[sc_i:sc_i + 1, :]) + mod_ref[sh_i:sh_i + 1, :]


def _prenorm_kernel(x_ref, mod_ref, g_ref, o_ref, *, sh_i, sc_i, pack):
    h = _modulated_norm(x_ref, mod_ref, g_ref, sh_i, sc_i)
    o_ref[...] = _pack_pair(h) if pack else h.astype(BF16)


def _route_kernel(x_ref, mod_ref, g_ref, wr_ref, info_ref, cnt_ref, carry, *, sh_i, sc_i, n_exp):
    i = pl.program_id(0)

    @pl.when(i == 0)
    def _():
        carry[...] = jnp.zeros_like(carry)

    h = _modulated_norm(x_ref, mod_ref, g_ref, sh_i, sc_i)
    w = wr_ref[...]
    hh = h.astype(BF16)
    hl = (h - hh.astype(F32)).astype(BF16)
    wh = w.astype(BF16)
    wl = (w - wh.astype(F32)).astype(BF16)
    logits = _dot(hh, wh) + _dot(hl, wh) + _dot(hh, wl)
    tm = logits.shape[0]
    lane = lax.broadcasted_iota(jnp.int32, (tm, LANES), 1).astype(F32)
    neg = jnp.float32(-jnp.inf)
    l1 = jnp.where(lane < n_exp, logits, neg)
    m1 = jnp.max(l1, axis=-1, keepdims=True)
    i1 = jnp.min(jnp.where(l1 == m1, lane, float(LANES)), axis=-1, keepdims=True)
    l2 = jnp.where(lane == i1, neg, l1)
    m2 = jnp.max(l2, axis=-1, keepdims=True)
    i2 = jnp.min(jnp.where(l2 == m2, lane, float(LANES)), axis=-1, keepdims=True)
    e = jnp.exp(m2 - m1)
    w1 = 1.0 / (1.0 + e)
    w2 = e / (1.0 + e)
    sel1 = lane == i1
    sel2 = lane == i2
    a = jnp.where(sel1 | sel2, 1.0, 0.0)
    r = lax.broadcasted_iota(jnp.int32, (tm, tm), 0)
    c = lax.broadcasted_iota(jnp.int32, (tm, tm), 1)
    before = jnp.where(c < r, 1.0, 0.0).astype(BF16)
    pos = _dot(before, a.astype(BF16)) + carry[...]
    p1 = jnp.sum(jnp.where(sel1, pos, 0.0), axis=-1, keepdims=True)
    p2 = jnp.sum(jnp.where(sel2, pos, 0.0), axis=-1, keepdims=True)
    total = carry[...] + jnp.sum(a, axis=0, keepdims=True)
    carry[...] = total
    cnt_ref[...] = jnp.broadcast_to(total, cnt_ref.shape)
    info = jnp.where(lane == 0, i1, 0.0)
    info = jnp.where(lane == 1, i2, info)
    info = jnp.where(lane == 2, p1, info)
    info = jnp.where(lane == 3, p2, info)
    info = jnp.where(lane == 4, w1, info)
    info = jnp.where(lane == 5, w2, info)
    info_ref[...] = info


def _mod_spec(d, tm, row_of):
    return pl.BlockSpec((None, 6, d), lambda i: (row_of(i * tm), 0, 0))


def _prenorm(x, mod, g, row_of, sh_i, sc_i, pack):
    t, d = x.shape
    tm = row_of.tile(512)
    out = jax.ShapeDtypeStruct((t, d // 2), jnp.uint32) if pack else jax.ShapeDtypeStruct((t, d), BF16)
    ow = d // 2 if pack else d
    return pl.pallas_call(
        functools.partial(_prenorm_kernel, sh_i=sh_i, sc_i=sc_i, pack=pack),
        grid=(t // tm,),
        in_specs=[pl.BlockSpec((tm, d), lambda i: (i, 0)),
                  _mod_spec(d, tm, row_of),
                  pl.BlockSpec((1, d), lambda i: (0, 0))],
        out_specs=pl.BlockSpec((tm, ow), lambda i: (i, 0)),
        out_shape=out,
        compiler_params=_params("parallel"),
        name="prenorm",
    )(x, mod, g.reshape(1, d))


def _route(x, mod, g, w_router, row_of, sh_i, sc_i):
    t, d = x.shape
    n_exp = w_router.shape[1]
    tm = row_of.tile(512)
    wr = jnp.zeros((d, LANES), F32).at[:, :n_exp].set(w_router)
    return pl.pallas_call(
        functools.partial(_route_kernel, sh_i=sh_i, sc_i=sc_i, n_exp=n_exp),
        grid=(t // tm,),
        in_specs=[pl.BlockSpec((tm, d), lambda i: (i, 0)),
                  _mod_spec(d, tm, row_of),
                  pl.BlockSpec((1, d), lambda i: (0, 0)),
                  pl.BlockSpec((d, LANES), lambda i: (0, 0))],
        out_specs=[pl.BlockSpec((tm, LANES), lambda i: (i, 0)),
                   pl.BlockSpec((8, LANES), lambda i: (0, 0))],
        out_shape=[jax.ShapeDtypeStruct((t, LANES), F32),
                   jax.ShapeDtypeStruct((8, LANES), F32)],
        scratch_shapes=[pltpu.VMEM((1, LANES), F32)],
        compiler_params=_params("arbitrary"),
        name="route",
    )(x, mod, g.reshape(1, d), wr)


IN_PROJ_CHUNKS = 4


def _in_proj_kernel(x0_ref, mod0_ref, xc_ref, modn_ref, g_ref, w_ref, o_ref, h_even, h_odd, *, sh_i, sc_i):
    i = pl.program_id(0)
    jc = jnp.minimum(pl.program_id(1), IN_PROJ_CHUNKS - 1)
    tc = xc_ref.shape[0]

    @pl.when((i == 0) & (pl.program_id(1) == 0))
    def _():
        h_even[...] = _modulated_norm(x0_ref, mod0_ref, g_ref, sh_i, sc_i).astype(BF16)

    def step(h_cur, h_next):
        h_next[pl.ds(pl.multiple_of(jc * tc, tc), tc), :] = (
            _modulated_norm(xc_ref, modn_ref, g_ref, sh_i, sc_i).astype(BF16))
        o_ref[...] = _dot(h_cur[...], w_ref[...]).astype(o_ref.dtype)

    @pl.when(i % 2 == 0)
    def _():
        step(h_even, h_odd)

    @pl.when(i % 2 == 1)
    def _():
        step(h_odd, h_even)


def _in_proj(x, mod, g, w, layer, row_of, sh_i, sc_i):
    t, d = x.shape
    n = w.shape[2]
    tm = row_of.tile(1024)
    tn = _pick(n, 1664, LANES)
    ni = t // tm
    tc = tm // IN_PROJ_CHUNKS
    assert n // tn >= IN_PROJ_CHUNKS

    def nxt(i):
        return jnp.minimum(i + 1, ni - 1)

    return pl.pallas_call(
        functools.partial(_in_proj_kernel, sh_i=sh_i, sc_i=sc_i),
        grid=(ni, n // tn),
        in_specs=[pl.BlockSpec((tm, d), lambda i, j: (0, 0), pipeline_mode=pl.Buffered(1)),
                  pl.BlockSpec((None, 6, d), lambda i, j: (row_of(0), 0, 0)),
                  pl.BlockSpec((tc, d), lambda i, j: (nxt(i) * IN_PROJ_CHUNKS
                                                      + jnp.minimum(j, IN_PROJ_CHUNKS - 1), 0)),
                  pl.BlockSpec((None, 6, d), lambda i, j: (row_of(nxt(i) * tm), 0, 0)),
                  pl.BlockSpec((1, d), lambda i, j: (0, 0)),
                  pl.BlockSpec((None, d, tn), lambda i, j: (layer, 0, j))],
        out_specs=pl.BlockSpec((tm, tn), lambda i, j: (i, j)),
        out_shape=jax.ShapeDtypeStruct((t, n), BF16),
        scratch_shapes=[pltpu.VMEM((tm, d), BF16), pltpu.VMEM((tm, d), BF16)],
        compiler_params=_params("arbitrary", "arbitrary"),
        name="in_proj",
    )(x, mod, x, mod, g.reshape(1, d), w)


def _gla_kernel(*refs, n_rows, seqs, has_s0, has_prev, want_state):
    q_ref, k_ref, v_ref, r_ref, alr_ref, wal_ref, bal_ref, gh_ref = refs[:8]
    pos = 8
    s0_ref = None
    if has_s0:
        s0_ref = refs[pos]
        pos += 1
    if has_prev:
        pos += 1
    og_ref = refs[pos]
    pos += 1
    sfin_ref = None
    if want_state:
        sfin_ref = refs[pos]
        pos += 1
    of_scr, ob_scr, st_scr = refs[pos:pos + 3]
    o_scr = (of_scr, ob_scr)

    dk = q_ref.shape[1]
    n_slab = n_rows // SLAB
    n_chunk = SLAB // GLA_CHUNK
    scale = dk ** -0.5

    ri = lax.broadcasted_iota(jnp.int32, (SLAB, SLAB), 0)
    ci = lax.broadcasted_iota(jnp.int32, (SLAB, SLAB), 1)
    shift = GLA_CHUNK.bit_length() - 1
    same = jnp.right_shift(ri, shift) == jnp.right_shift(ci, shift)
    keep = (same & (ci <= ri), same & (ci >= ri))
    sum_mats = tuple(jnp.where(m, 1.0, 0.0).astype(BF16) for m in keep)

    for s in range(seqs):
        for d in range(2):
            if has_s0:
                st_scr[s, d] = s0_ref[d].T
            else:
                st_scr[s, d] = jnp.zeros(st_scr.shape[2:], F32)

    group = math.gcd(n_slab, GLA_GROUP)
    order = (tuple(range(n_chunk)), tuple(range(n_chunk - 1, -1, -1)))
    chunk_rows = [slice(c * GLA_CHUNK, (c + 1) * GLA_CHUNK) for c in range(n_chunk)]

    def body(i, carry):
        jobs = []
        for s in range(seqs):
            jobs += [(0, pl.multiple_of(s * n_rows + (i * group + g) * SLAB, SLAB), s) for g in range(group)]
            jobs += [(1, pl.multiple_of(s * n_rows + (n_slab - 1 - (i * group + g)) * SLAB, SLAB), s)
                     for g in range(group)]
        cols = [slice(d * dk, (d + 1) * dk) for d, _, _ in jobs]
        rows = [pl.ds(r0, SLAB) for _, r0, _ in jobs]
        z = [_dot(alr_ref[rw, :], wal_ref[:, cl]) + bal_ref[:, cl] for rw, cl in zip(rows, cols)]
        parts = []
        for zz in z:
            la = (jnp.minimum(zz, 0.0) - jnp.log1p(jnp.exp(-jnp.abs(zz)))) * (1.0 / GLA_TAU)
            hi = la.astype(BF16)
            rem = la - hi.astype(F32)
            mid = rem.astype(BF16)
            lo = (rem - mid.astype(F32)).astype(BF16)
            parts.append(jnp.concatenate([hi, mid, lo], axis=1))
        ct = [_dot(sum_mats[d], p) for (d, _, _), p in zip(jobs, parts)]
        qd, ki, ke, dec, v = [], [], [], [], []
        for (d, _, _), rw, c3 in zip(jobs, rows, ct):
            cum = c3[:, :dk] + c3[:, dk:2 * dk] + c3[:, 2 * dk:]
            edge = GLA_CHUNK - 1 if d == 0 else 0
            tot_rows = [cum[c * GLA_CHUNK + edge:c * GLA_CHUNK + edge + 1, :] for c in range(n_chunk)]
            tot = jnp.concatenate([jnp.broadcast_to(r, (GLA_CHUNK, dk)) for r in tot_rows], axis=0)
            q = q_ref[rw, :].astype(F32) * scale
            k = k_ref[rw, :].astype(F32)
            qd.append((q * jnp.exp(cum)).astype(BF16))
            ki.append((k * jnp.exp(-cum)).astype(BF16))
            ke.append((k * jnp.exp(tot - cum)).astype(BF16))
            dec.append([jnp.exp(r) for r in tot_rows])
            v.append(v_ref[rw, :])
        n_jobs = len(jobs)
        scores = [_dot_nt(qd[n], ki[n]) for n in range(n_jobs)]
        kv = [[_dot_tn(v[n][sl], ke[n][sl]) for sl in chunk_rows] for n in range(n_jobs)]
        o = [_dot(jnp.where(keep[jobs[n][0]], scores[n], 0.0).astype(BF16), v[n]) for n in range(n_jobs)]
        st_in = [[None] * n_chunk for _ in range(n_jobs)]
        for s in range(seqs):
            for d in range(2):
                st = st_scr[s, d]
                for n in range(n_jobs):
                    if jobs[n][0] != d or jobs[n][2] != s:
                        continue
                    for c in order[d]:
                        st_in[n][c] = st.astype(BF16)
                        st = dec[n][c] * st + kv[n][c]
                st_scr[s, d] = st
        for n, (d, r0, _) in enumerate(jobs):
            for c in order[d]:
                o_inter = _dot_nt(qd[n][chunk_rows[c]], st_in[n][c])
                o_scr[d][pl.ds(r0 + c * GLA_CHUNK, GLA_CHUNK), :] = o[n][chunk_rows[c]] + o_inter
        return carry

    lax.fori_loop(0, n_slab // group, body, 0)

    def finish(i, carry):
        rows = pl.ds(pl.multiple_of(i * SLAB, SLAB), SLAB)
        o = of_scr[rows, :] + ob_scr[rows, :]
        o = o * lax.rsqrt(jnp.mean(o * o, axis=-1, keepdims=True) + EPS) * gh_ref[...]
        og_ref[rows, :] = (o * _silu(r_ref[rows, :].astype(F32))).astype(og_ref.dtype)
        return carry

    lax.fori_loop(0, seqs * n_slab, finish, 0)
    if want_state:
        for s in range(seqs):
            for d in range(2):
                sfin_ref[s, d] = st_scr[s, d].T


def _gla(u, wal, bal, gh, s0, layer, prev, *, row0, n_seq, n_rows, col, want_state):
    t = u.shape[0]
    heads, dk2 = wal.shape[0], wal.shape[2]
    dk = dk2 // 2
    dv = gh.shape[2]
    has_s0 = s0 is not None
    has_prev = prev is not None
    seqs = 1 if has_s0 else math.gcd(n_seq, GLA_GROUP // math.gcd(n_rows // SLAB, GLA_GROUP))
    br = seqs * n_rows
    assert row0 % br == 0 and n_rows % SLAB == 0
    rb0 = row0 // br

    def at(col_units):
        return lambda b, h: (rb0 + b, col_units + h)

    in_specs = [pl.BlockSpec((br, dk), at(col["q"] // dk)),
                pl.BlockSpec((br, dk), at(col["k"] // dk)),
                pl.BlockSpec((br, dv), at(col["v"] // dv)),
                pl.BlockSpec((br, dv), at(col["r"] // dv)),
                pl.BlockSpec((br, LANES), lambda b, h: (rb0 + b, col["alr"] // LANES)),
                pl.BlockSpec((None, LANES, dk2), lambda b, h: (h, 0, 0)),
                pl.BlockSpec((None, 1, dk2), lambda b, h: (h, 0, 0)),
                pl.BlockSpec((None, 1, dv), lambda b, h: (h, 0, 0))]
    args = [u, u, u, u, u, wal, bal, gh]
    if has_s0:
        in_specs.append(pl.BlockSpec((None, None, 2, None, dk, dv), lambda b, h: (b, layer, 0, h, 0, 0)))
        args.append(s0)
    aliases = {}
    if has_prev:
        aliases = {len(args): 0}
        in_specs.append(pl.BlockSpec(memory_space=pl.ANY))
        args.append(prev)
    out_specs = [pl.BlockSpec((br, dv), lambda b, h: (rb0 + b, h))]
    out_shape = [jax.ShapeDtypeStruct((t, heads * dv), BF16)]
    if want_state:
        out_specs.append(pl.BlockSpec((seqs, 2, None, dk, dv), lambda b, h: (b, 0, h, 0, 0)))
        out_shape.append(jax.ShapeDtypeStruct((n_seq, 2, heads, dk, dv), F32))
    res = pl.pallas_call(
        functools.partial(_gla_kernel, n_rows=n_rows, seqs=seqs, has_s0=has_s0, has_prev=has_prev,
                          want_state=want_state),
        grid=(n_seq // seqs, heads),
        in_specs=in_specs,
        out_specs=out_specs,
        out_shape=out_shape,
        input_output_aliases=aliases,
        scratch_shapes=[pltpu.VMEM((br, dv), F32), pltpu.VMEM((br, dv), F32),
                        pltpu.VMEM((seqs, 2, dv, dk), F32)],
        compiler_params=_params("parallel", "parallel"),
        name="gla",
    )(*args)
    return res if want_state else (res[0], None)


def _chan_dft_kernel(x_ref, tab_ref, xc_ref, xs_ref):
    y = _dot(x_ref[...], tab_ref[...])
    w = xc_ref.shape[1]
    xc_ref[...] = y[:, :w].astype(xc_ref.dtype)
    xs_ref[...] = y[:, w:].astype(xs_ref.dtype)


def _chan_dft(u, tab, col_f, width, row0, n_rows):
    gw = tab.shape[0]
    tm = _pick(math.gcd(row0, n_rows) if row0 else n_rows, 2048)
    c0 = col_f // gw
    i0 = row0 // tm
    spec_o = pl.BlockSpec((tm, gw), lambda i, g: (i, g))
    return pl.pallas_call(
        _chan_dft_kernel,
        grid=(n_rows // tm, width // gw),
        in_specs=[pl.BlockSpec((tm, gw), lambda i, g: (i0 + i, c0 + g)),
                  pl.BlockSpec((gw, 2 * gw), lambda i, g: (0, 0))],
        out_specs=[spec_o, spec_o],
        out_shape=[jax.ShapeDtypeStruct((n_rows, width), BF16)] * 2,
        compiler_params=_params("parallel", "parallel"),
        name="chan_dft",
    )(u, tab)


def _chan_fold_kernel(xm_ref, xa_ref, xt_ref, xh_ref, tab_ref, ap_ref, bm_ref, mid_ref):
    m = pl.program_id(1)
    blk, width = xm_ref.shape
    gw = tab_ref.shape[0]
    r = lax.broadcasted_iota(jnp.int32, (blk, blk), 0)
    c = lax.broadcasted_iota(jnp.int32, (blk, blk), 1)
    flip = jnp.where(r + c == blk, 1.0, 0.0).astype(BF16)
    rev = _dot(flip, xa_ref[...])
    first = jnp.where(m > 0, xt_ref[0:1, :].astype(F32), 0.0)
    rev = jnp.where(lax.broadcasted_iota(jnp.int32, (blk, width), 0) == 0, first, rev)
    xm = xm_ref[...].astype(F32)
    plus = (xm + rev).astype(BF16)
    minus = (xm - rev).astype(BF16)
    xh = xh_ref[...]
    tab = tab_ref[...]
    for g in range(width // gw):
        cs = slice(g * gw, (g + 1) * gw)
        ap_ref[:, cs] = _dot(plus[:, cs], tab[:, :gw]).astype(ap_ref.dtype)
        bm_ref[:, cs] = _dot(minus[:, cs], tab[:, gw:]).astype(bm_ref.dtype)
        mid_ref[:, cs] = _dot(xh[:, cs], tab[:, :gw])


def _chan_fold(u, tab, col_f, width, row0, n_seq, n_rows):
    gw = tab.shape[0]
    blk = SLAB
    sub = 16
    nb = n_rows // blk
    nbh = nb // 2
    assert row0 % blk == 0 and n_rows % (2 * blk) == 0
    base = row0 // blk
    assert col_f % width == 0
    c0 = col_f // width

    def first_of(b, m):
        return base + b * nb + jnp.where(m == 0, 0, nb - m)

    half_rows = n_seq * n_rows // 2
    spec_o = pl.BlockSpec((blk, width), lambda b, m: (b * nbh + m, 0))
    return pl.pallas_call(
        _chan_fold_kernel,
        grid=(n_seq, nbh),
        in_specs=[pl.BlockSpec((blk, width), lambda b, m: (base + b * nb + m, c0)),
                  pl.BlockSpec((blk, width), lambda b, m: (base + b * nb + nb - 1 - m, c0)),
                  pl.BlockSpec((sub, width), lambda b, m: (first_of(b, m) * (blk // sub), c0)),
                  pl.BlockSpec((sub, width), lambda b, m: ((base + b * nb + nbh) * (blk // sub), c0)),
                  pl.BlockSpec((gw, 2 * gw), lambda b, m: (0, 0))],
        out_specs=[spec_o, spec_o, pl.BlockSpec((None, sub, width), lambda b, m: (b, 0, 0))],
        out_shape=[jax.ShapeDtypeStruct((half_rows, width), BF16), jax.ShapeDtypeStruct((half_rows, width), BF16),
                   jax.ShapeDtypeStruct((n_seq, sub, width), F32)],
        compiler_params=_params("parallel", "arbitrary"),
        name="chan_fold",
    )(u, u, u, u, tab)


def _pos_dft_kernel(*refs, fold):
    tc_ref, ts_ref, xc_ref, xs_ref = refs[:4]
    o_ref = refs[-1]
    y = _dot(tc_ref[...], xc_ref[...]) + _dot(ts_ref[...], xs_ref[...])
    if fold:
        sign_ref, mid_ref = refs[4:6]
        y = y + sign_ref[...] * mid_ref[0:1, :]
    o_ref[...] = y.astype(o_ref.dtype)


def _pos_dft(tc, ts, xc, xs, prev, *, row0, n_seq, n_rows, sign=None, mid=None):
    t, width = prev.shape
    k_rows = tc.shape[1]
    fold = sign is not None
    assert row0 % n_rows == 0
    rb0 = row0 // n_rows
    to = _pick(n_rows, 512)
    nt = n_rows // to
    once = pl.Buffered(1)
    in_specs = [pl.BlockSpec((to, k_rows), lambda b, i: (i, 0)),
                pl.BlockSpec((to, k_rows), lambda b, i: (i, 0)),
                pl.BlockSpec((k_rows, width), lambda b, i: (b, 0), pipeline_mode=once),
                pl.BlockSpec((k_rows, width), lambda b, i: (b, 0), pipeline_mode=once)]
    args = [tc, ts, xc, xs]
    if fold:
        in_specs += [pl.BlockSpec((to, 1), lambda b, i: (i, 0)),
                     pl.BlockSpec((None,) + mid.shape[1:], lambda b, i: (b, 0, 0))]
        args += [sign, mid]
    in_specs.append(pl.BlockSpec(memory_space=pl.ANY))
    args.append(prev)
    return pl.pallas_call(
        functools.partial(_pos_dft_kernel, fold=fold),
        grid=(n_seq, nt),
        in_specs=in_specs,
        out_specs=pl.BlockSpec((to, width), lambda b, i: ((rb0 + b) * nt + i, 0)),
        out_shape=jax.ShapeDtypeStruct((t, width), BF16),
        input_output_aliases={len(args) - 1: 0},
        compiler_params=_params("parallel", "parallel"),
        name="pos_dft",
    )(*args)


def _dft_tables(n, scale, cols=None, split=64):
    cols = n if cols is None else cols
    k = jnp.arange(cols, dtype=jnp.int32)

    def direct(j):
        ang = ((j[:, None] * k[None, :]) % n).astype(F32) * (2.0 * math.pi / n)
        return jnp.cos(ang), jnp.sin(ang)

    if n <= 4 * split or n % split:
        c, s = direct(jnp.arange(n, dtype=jnp.int32))
    else:
        ca, sa = direct(jnp.arange(0, n, split, dtype=jnp.int32))
        cb, sb = direct(jnp.arange(split, dtype=jnp.int32))
        c = (ca[:, None, :] * cb[None] - sa[:, None, :] * sb[None]).reshape(n, cols)
        s = (sa[:, None, :] * cb[None] + ca[:, None, :] * sb[None]).reshape(n, cols)
    return c * scale, s * scale


def _mix_out_kernel(og_ref, fr_ref, ga_ref, gb_ref, x_ref, mod_ref, g_ref, wa_ref, wb_ref, wo_ref,
                    o_ref, *, ga_i):
    ba = _dot(og_ref[...], wa_ref[...])
    bb = _dot(fr_ref[...], wb_ref[...])
    merged = _sigmoid(ga_ref[...].astype(F32)) * ba + _sigmoid(gb_ref[...].astype(F32)) * bb
    out = _dot(merged.astype(BF16), wo_ref[...])
    y = out * lax.rsqrt(jnp.mean(out * out, axis=-1, keepdims=True) + EPS) * g_ref[...]
    o_ref[...] = x_ref[...] + mod_ref[ga_i:ga_i + 1, :] * y


def _mix_out(og, fr, u, x, mod, g, wa, wb, wo, layer, row_of, col, ga_i):
    t, d = x.shape
    w = og.shape[1]
    tm = row_of.tile(512)
    once = pl.Buffered(1)
    return pl.pallas_call(
        functools.partial(_mix_out_kernel, ga_i=ga_i),
        grid=(t // tm,),
        in_specs=[pl.BlockSpec((tm, w), lambda i: (i, 0)),
                  pl.BlockSpec((tm, w), lambda i: (i, 0)),
                  pl.BlockSpec((tm, d), lambda i: (i, col["ga"] // d)),
                  pl.BlockSpec((tm, d), lambda i: (i, col["gb"] // d)),
                  pl.BlockSpec((tm, d), lambda i: (i, 0)),
                  _mod_spec(d, tm, row_of),
                  pl.BlockSpec((1, d), lambda i: (0, 0)),
                  pl.BlockSpec((None, w, d), lambda i: (layer, 0, 0), pipeline_mode=once),
                  pl.BlockSpec((None, w, d), lambda i: (layer, 0, 0), pipeline_mode=once),
                  pl.BlockSpec((None, d, d), lambda i: (layer, 0, 0), pipeline_mode=once)],
        out_specs=pl.BlockSpec((tm, d), lambda i: (i, 0)),
        out_shape=jax.ShapeDtypeStruct((t, d), F32),
        compiler_params=_params("parallel"),
        name="mix_out",
    )(og, fr, u, u, x, mod, g.reshape(1, d), wa, wb, wo)


def _ffn_kernel(te_ref, nu_ref, x_ref, wg_ref, wu_ref, wd_ref, o_ref, xb, acc):
    g = pl.program_id(0)
    j = pl.program_id(1)
    nj = pl.num_programs(1)
    half = x_ref.shape[1]
    used = g < nu_ref[0]

    @pl.when(used & (j == 0))
    def _():
        lo, hi = _unpack_pair(x_ref[...])
        xb[:, :half] = lo.astype(BF16)
        xb[:, half:] = hi.astype(BF16)
        acc[...] = jnp.zeros_like(acc)

    @pl.when(used)
    def _():
        x = xb[...]
        a = _silu(_dot(x, wg_ref[...])) * _dot(x, wu_ref[...])
        acc[...] += _dot(a.astype(BF16), wd_ref[...])

    @pl.when(j == nj - 1)
    def _():
        @pl.when(used)
        def _():
            o_ref[...] = _pack_pair(acc[...])

        @pl.when(jnp.logical_not(used))
        def _():
            o_ref[...] = jnp.zeros_like(o_ref)


def _ffn(xu, tile_expert, n_used, wg, wu, wd, layer, tm):
    s, half = xu.shape
    _, _, d, f = wg.shape
    tf = _pick(f, 512, LANES)
    nj = f // tf
    n_tiles = s // tm

    def jj(g, j, nu):
        return jnp.where(g < nu[0], j, nj - 1)

    grid_spec = pltpu.PrefetchScalarGridSpec(
        num_scalar_prefetch=2,
        grid=(n_tiles, nj),
        in_specs=[pl.BlockSpec((tm, half), lambda g, j, te, nu: (g, 0)),
                  pl.BlockSpec((None, None, d, tf), lambda g, j, te, nu: (layer, te[g], 0, jj(g, j, nu))),
                  pl.BlockSpec((None, None, d, tf), lambda g, j, te, nu: (layer, te[g], 0, jj(g, j, nu))),
                  pl.BlockSpec((None, None, tf, d), lambda g, j, te, nu: (layer, te[g], jj(g, j, nu), 0))],
        out_specs=pl.BlockSpec((tm, half), lambda g, j, te, nu: (g, 0)),
        scratch_shapes=[pltpu.VMEM((tm, d), BF16), pltpu.VMEM((tm, d), F32)])
    return pl.pallas_call(
        _ffn_kernel,
        grid_spec=grid_spec,
        out_shape=jax.ShapeDtypeStruct((s, half), jnp.uint32),
        compiler_params=_params("parallel", "arbitrary"),
        name="ffn",
    )(tile_expert, n_used, xu, wg, wu, wd)


def _post_norm_residual(y, x_ref, mod_ref, g_ref, o_ref, ga_i):
    yn = y * lax.rsqrt(jnp.mean(y * y, axis=-1, keepdims=True) + EPS) * g_ref[...]
    o_ref[...] = x_ref[...] + mod_ref[ga_i:ga_i + 1, :] * yn


def _post_kernel(y_ref, x_ref, mod_ref, g_ref, o_ref, *, ga_i):
    lo, hi = _unpack_pair(y_ref[...])
    _post_norm_residual(jnp.concatenate([lo, hi], axis=1), x_ref, mod_ref, g_ref, o_ref, ga_i)


def _post(yu, x, mod, g, row_of, ga_i, row0=0, n_rows=None):
    t, d = x.shape
    n_rows = t if n_rows is None else n_rows
    tm = row_of.tile(512)
    b0 = row0 // tm
    return pl.pallas_call(
        functools.partial(_post_kernel, ga_i=ga_i),
        grid=(n_rows // tm,),
        in_specs=[pl.BlockSpec((tm, d // 2), lambda i: (b0 + i, 0)),
                  pl.BlockSpec((tm, d), lambda i: (b0 + i, 0)),
                  pl.BlockSpec((None, 6, d), lambda i: (row_of((b0 + i) * tm), 0, 0)),
                  pl.BlockSpec((1, d), lambda i: (0, 0))],
        out_specs=pl.BlockSpec((tm, d), lambda i: (i, 0)),
        out_shape=jax.ShapeDtypeStruct((n_rows, d), F32),
        compiler_params=_params("parallel"),
        name="post",
    )(yu, x, mod, g.reshape(1, d))


def _row_copy(src_ref, src_row, dst_ref, dst_row, sem):
    return pltpu.make_async_copy(src_ref.at[pl.ds(src_row, 1), :], dst_ref.at[pl.ds(dst_row, 1), :], sem)


def _dispatch_kernel(s1_ref, s2_ref, x_ref, mod_ref, g_ref, init_ref, xs_ref, buf, sem, *, tb, sh_i, sc_i):
    del init_ref
    i = pl.program_id(0)
    n = pl.num_programs(0)
    cur = i % 2

    def drain(b):
        whole = pltpu.make_async_copy(buf.at[b], xs_ref.at[pl.ds(0, tb), :], sem.at[b])
        whole.wait()
        whole.wait()

    @pl.when(i >= 2)
    def _():
        drain(cur)

    buf[cur] = _pack_pair(_modulated_norm(x_ref, mod_ref, g_ref, sh_i, sc_i))

    def issue(t, carry):
        _row_copy(buf.at[cur], t, xs_ref, s1_ref[0, 0, t], sem.at[cur]).start()
        _row_copy(buf.at[cur], t, xs_ref, s2_ref[0, 0, t], sem.at[cur]).start()
        return carry

    lax.fori_loop(0, tb, issue, 0, unroll=8)

    @pl.when(i == n - 1)
    def _():
        drain(cur)

        @pl.when(n >= 2)
        def _():
            drain(1 - cur)


def _dispatch(x, mod, g, slot1, slot2, n_slots, row_of, sh_i, sc_i):
    t, d = x.shape
    half = d // 2
    tb = row_of.tile(512)
    nb = t // tb
    smem = functools.partial(pl.BlockSpec, (1, 1, tb), lambda i: (i, 0, 0), memory_space=pltpu.SMEM)
    return pl.pallas_call(
        functools.partial(_dispatch_kernel, tb=tb, sh_i=sh_i, sc_i=sc_i),
        grid=(nb,),
        in_specs=[smem(), smem(),
                  pl.BlockSpec((tb, d), lambda i: (i, 0)),
                  _mod_spec(d, tb, row_of),
                  pl.BlockSpec((1, d), lambda i: (0, 0)),
                  pl.BlockSpec(memory_space=pl.ANY)],
        out_specs=pl.BlockSpec(memory_space=pl.ANY),
        out_shape=jax.ShapeDtypeStruct((n_slots, half), jnp.uint32),
        scratch_shapes=[pltpu.VMEM((2, tb, half), jnp.uint32), pltpu.SemaphoreType.DMA((2,))],
        input_output_aliases={5: 0},
        compiler_params=_params("arbitrary"),
        name="dispatch",
    )(slot1.reshape(nb, 1, tb), slot2.reshape(nb, 1, tb), x, mod, g.reshape(1, d),
      jnp.zeros((n_slots, half), jnp.uint32))


def _combine_kernel(s1_ref, s2_ref, info_ref, x_ref, mod_ref, g_ref, ys_ref, o_ref, b1, b2, sem, *,
                    tb, ga_i):
    def issue(t, carry):
        _row_copy(ys_ref, s1_ref[0, 0, t], b1, t, sem).start()
        _row_copy(ys_ref, s2_ref[0, 0, t], b2, t, sem).start()
        return carry

    lax.fori_loop(0, tb, issue, 0, unroll=8)

    pltpu.make_async_copy(ys_ref.at[pl.ds(0, tb), :], b1, sem).wait()
    pltpu.make_async_copy(ys_ref.at[pl.ds(0, tb), :], b2, sem).wait()
    info = info_ref[...]
    w1 = info[:, 4:5]
    w2 = info[:, 5:6]
    lo1, hi1 = _unpack_pair(b1[...])
    lo2, hi2 = _unpack_pair(b2[...])
    y = jnp.concatenate([w1 * lo1 + w2 * lo2, w1 * hi1 + w2 * hi2], axis=1)
    _post_norm_residual(y, x_ref, mod_ref, g_ref, o_ref, ga_i)


def _combine(ys, slot1, slot2, info, x, mod, g, row_of, ga_i, row0=0, n_rows=None):
    t, d = x.shape
    n_rows = t if n_rows is None else n_rows
    half = d // 2
    tb = row_of.tile(512)
    nb = t // tb
    b0 = row0 // tb
    smem = functools.partial(pl.BlockSpec, (1, 1, tb), lambda i: (b0 + i, 0, 0), memory_space=pltpu.SMEM)
    return pl.pallas_call(
        functools.partial(_combine_kernel, tb=tb, ga_i=ga_i),
        grid=(n_rows // tb,),
        in_specs=[smem(), smem(),
                  pl.BlockSpec((tb, LANES), lambda i: (b0 + i, 0)),
                  pl.BlockSpec((tb, d), lambda i: (b0 + i, 0)),
                  pl.BlockSpec((None, 6, d), lambda i: (row_of((b0 + i) * tb), 0, 0)),
                  pl.BlockSpec((1, d), lambda i: (0, 0)),
                  pl.BlockSpec(memory_space=pl.ANY)],
        out_specs=pl.BlockSpec((tb, d), lambda i: (i, 0)),
        out_shape=jax.ShapeDtypeStruct((n_rows, d), F32),
        scratch_shapes=[pltpu.VMEM((tb, half), jnp.uint32), pltpu.VMEM((tb, half), jnp.uint32),
                        pltpu.SemaphoreType.DMA(())],
        compiler_params=_params("arbitrary"),
        name="combine",
    )(slot1.reshape(nb, 1, tb), slot2.reshape(nb, 1, tb), info, x, mod, g.reshape(1, d), ys)


def _route_plan(info, counts, n_exp, tm, n_tiles):
    cnt = counts[0, :n_exp].astype(jnp.int32)
    tiles = (cnt + tm - 1) // tm
    ends = jnp.cumsum(tiles)
    base = (ends - tiles) * tm
    i1 = info[:, 0].astype(jnp.int32)
    i2 = info[:, 1].astype(jnp.int32)
    slot1 = base[i1] + info[:, 2].astype(jnp.int32)
    slot2 = base[i2] + info[:, 3].astype(jnp.int32)
    tile_ids = jnp.arange(n_tiles, dtype=jnp.int32)
    tile_expert = jnp.minimum(jnp.sum((ends[None, :] <= tile_ids[:, None]).astype(jnp.int32), axis=1), n_exp - 1)
    return slot1, slot2, tile_expert, ends[-1:].astype(jnp.int32)


def kernel(x_prompt, x_sample, state_gla, c, c_ctx, w_ada, b_ada, g_mix_pre, g_mix_post, w_in, w_alpha_up, b_alpha, g_head, w_gla_br, w_fn_br, w_out, g_ffn_pre, g_ffn_post, w_ffn_gate, w_ffn_up, w_ffn_down, w_router, w_exp_gate, w_exp_up, w_exp_down):
    bp, np_, d = x_prompt.shape
    bs, ns_, _ = x_sample.shape
    depth = w_ada.shape[0]
    heads = GLA_HEADS
    qk = w_alpha_up.shape[-1]
    dk = qk // heads
    vw = g_head.shape[-1]
    dv = vw // heads
    fw = w_fn_br.shape[1]
    gw = fw // FN_GROUPS
    n_exp = w_router.shape[-1]
    p_rows = bp * np_
    t = p_rows + bs * ns_
    assert dk == LANES and 2 * GLA_LR <= LANES and p_rows % ns_ == 0 and ns_ % np_ == 0

    row_of = _Rows(p_rows, ns_)

    col = {"q": 0, "k": qk, "v": 2 * qk, "r": 2 * qk + vw, "f": 2 * qk + 2 * vw,
           "ga": 2 * qk + 2 * vw + fw, "gb": 2 * qk + 2 * vw + fw + d, "alr": 2 * qk + 2 * vw + fw + 2 * d}
    src_alr = 2 * qk + 2 * vw
    n_u = col["alr"] + LANES
    assert col["ga"] % d == 0 and col["gb"] % d == 0 and col["f"] % gw == 0

    n_cond = 1 + bs
    cond = jnp.zeros((16, d), F32).at[0].set(c_ctx).at[1:n_cond].set(c)
    mod_all = _ada(cond, w_ada, b_ada)[:, :n_cond].reshape(depth, n_cond, 6, d)

    chan_c, chan_s = _dft_tables(gw, gw ** -0.5)
    chan_tab = jnp.concatenate([chan_c, chan_s], axis=1).astype(BF16)
    pos_tabs = {}
    for n in (np_, ns_):
        kk = n if n % (2 * SLAB) else n // 2
        pc, ps = _dft_tables(n, n ** -0.5, kk)
        sign = jnp.where(jnp.arange(n) % 2 == 0, 1.0, -1.0).astype(F32).reshape(n, 1) * n ** -0.5
        pos_tabs[n] = (pc.astype(BF16), (-ps).astype(BF16), sign)

    def fourier(u, fr, row0, n_seq, n):
        tc, ts, sign = pos_tabs[n]
        if tc.shape[1] == n:
            xc, xs = _chan_dft(u, chan_tab, col["f"], fw, row0, n_seq * n)
            return _pos_dft(tc, ts, xc, xs, fr, row0=row0, n_seq=n_seq, n_rows=n)
        ap, bm, mid = _chan_fold(u, chan_tab, col["f"], fw, row0, n_seq, n)
        return _pos_dft(tc, ts, ap, bm, fr, row0=row0, n_seq=n_seq, n_rows=n, sign=sign, mid=mid)

    x = jnp.concatenate([x_prompt.reshape(p_rows, d), x_sample.reshape(bs * ns_, d)], axis=0)
    w_perm = jnp.concatenate(
        [w_in[:, :, :src_alr], w_in[:, :, src_alr + 2 * GLA_LR:], w_in[:, :, src_alr:src_alr + 2 * GLA_LR],
         jnp.zeros((depth, d, LANES - 2 * GLA_LR), F32)], axis=2).astype(BF16)
    wa_all, wb_all, wo_all = w_gla_br.astype(BF16), w_fn_br.astype(BF16), w_out.astype(BF16)
    dense_w = [w[:, None].astype(BF16) for w in (w_ffn_gate, w_ffn_up, w_ffn_down)]
    exp_w = [w.astype(BF16) for w in (w_exp_gate, w_exp_up, w_exp_down)]
    ctx_states = []
    for l in range(depth):
        mod = mod_all[l]
        u = _in_proj(x, mod, g_mix_pre[l], w_perm, l, row_of, 0, 1)
        wup = w_alpha_up[l].reshape(2, GLA_LR, heads, dk)
        wal = jnp.zeros((heads, LANES, 2 * dk), F32)
        wal = wal.at[:, :GLA_LR, :dk].set(wup[0].transpose(1, 0, 2))
        wal = wal.at[:, GLA_LR:2 * GLA_LR, dk:].set(wup[1].transpose(1, 0, 2)).astype(BF16)
        bal = b_alpha[l].reshape(2, heads, 1, dk).transpose(1, 2, 0, 3).reshape(heads, 1, 2 * dk)
        gh = g_head[l].reshape(heads, 1, dv)
        og, s_fin = _gla(u, wal, bal, gh, None, l, jnp.zeros((t, vw), BF16), row0=0, n_seq=bp, n_rows=np_,
                         col=col, want_state=True)
        og, _ = _gla(u, wal, bal, gh, state_gla, l, og, row0=p_rows, n_seq=bs, n_rows=ns_, col=col,
                     want_state=False)
        ctx_states.append(s_fin)
        fr = fourier(u, jnp.zeros((t, fw), BF16), 0, bp, np_)
        fr = fourier(u, fr, p_rows, bs, ns_)
        x = _mix_out(og, fr, u, x, mod, g_mix_post[l], wa_all, wb_all, wo_all, l, row_of, col, 2)
        j = l // 2
        if l % 2 == 0:
            hu = _prenorm(x, mod, g_ffn_pre[l], row_of, 3, 4, pack=True)
            tm = _pick(t, 1024)
            n_tiles = t // tm
            yu = _ffn(hu, jnp.zeros((n_tiles,), jnp.int32), jnp.full((1,), n_tiles, jnp.int32), *dense_w, j, tm)
            finish = functools.partial(_post, yu, x, mod, g_ffn_post[l], row_of, 5)
        else:
            info, counts = _route(x, mod, g_ffn_pre[l], w_router[j], row_of, 3, 4)
            tm = _pick(t, 1024)
            n_tiles = (TOP_K * t) // tm + n_exp
            slot1, slot2, tile_expert, n_used = _route_plan(info, counts, n_exp, tm, n_tiles)
            xs_rows = _dispatch(x, mod, g_ffn_pre[l], slot1, slot2, n_tiles * tm, row_of, 3, 4)
            ys_rows = _ffn(xs_rows, tile_expert, n_used, *exp_w, j, tm)
            finish = functools.partial(_combine, ys_rows, slot1, slot2, info, x, mod, g_ffn_post[l], row_of, 5)
        if l + 1 < depth:
            x = finish()
        else:
            y_prompt = finish(0, p_rows).reshape(bp, np_, d)
            y_sample = finish(p_rows, t - p_rows).reshape(bs, ns_, d)
    new_state = jnp.stack(ctx_states, axis=1).astype(x_prompt.dtype)
    return (y_prompt, y_sample, new_state)
```

```python
import functools
import math

import jax
import jax.numpy as jnp
from jax import lax
from jax.experimental import pallas as pl
from jax.experimental.pallas import tpu as pltpu

GLA_HEADS = 4
GLA_LR = 16
GLA_TAU = 16.0
GLA_CHUNK = 64
FN_GROUPS = 4
TOP_K = 2
EPS = 1e-6

LANES = 128
SLAB = 4 * GLA_CHUNK
GLA_GROUP = 4
VMEM_LIMIT = 56 * 1024 * 1024
BF16 = jnp.bfloat16
F32 = jnp.float32


def _pick(n, pref, mult=8):
    t = min(n, pref)
    t -= t % mult
    while t > mult and n % t:
        t -= mult
    assert t > 0 and n % t == 0, (n, pref)
    return t


class _Rows:
    def __init__(self, p_rows, seq_rows):
        self.p_rows, self.seq_rows = p_rows, seq_rows

    def __call__(self, start):
        return jnp.where(start < self.p_rows, 0, 1 + (start - self.p_rows) // self.seq_rows)

    def tile(self, pref):
        return _pick(math.gcd(self.p_rows, self.seq_rows), pref)


def _params(*sem):
    return pltpu.CompilerParams(dimension_semantics=sem, vmem_limit_bytes=VMEM_LIMIT)


def _silu(x):
    return x / (1.0 + jnp.exp(-x))


def _sigmoid(x):
    return 1.0 / (1.0 + jnp.exp(-x))


def _dot(a, b):
    return jnp.dot(a, b, preferred_element_type=F32)


def _dot_nt(a, b):
    return lax.dot_general(a, b, (((1,), (1,)), ((), ())), preferred_element_type=F32)


def _dot_tn(a, b):
    return lax.dot_general(a, b, (((0,), (0,)), ((), ())), preferred_element_type=F32)


def _pack_pair(h):
    half = h.shape[1] // 2
    bits = lax.bitcast_convert_type(h.astype(BF16).astype(F32), jnp.uint32)
    return (bits[:, half:] & jnp.uint32(0xFFFF0000)) | (bits[:, :half] >> 16)


def _unpack_pair(u):
    lo = lax.bitcast_convert_type(u << 16, F32)
    hi = lax.bitcast_convert_type(u & jnp.uint32(0xFFFF0000), F32)
    return lo, hi


def _ada_kernel(c_ref, w_ref, b_ref, o_ref):
    s = _silu(c_ref[...]).astype(BF16)
    o_ref[...] = _dot(s, w_ref[...].astype(BF16)) + b_ref[...]


def _ada(cond, w_ada, b_ada):
    depth, d, n6 = w_ada.shape
    rows = cond.shape[0]
    tn = _pick(n6, 1536, LANES)
    return pl.pallas_call(
        _ada_kernel,
        grid=(depth, n6 // tn),
        in_specs=[pl.BlockSpec((rows, d), lambda l, j: (0, 0)),
                  pl.BlockSpec((None, d, tn), lambda l, j: (l, 0, j)),
                  pl.BlockSpec((None, 1, tn), lambda l, j: (l, 0, j))],
        out_specs=pl.BlockSpec((None, rows, tn), lambda l, j: (l, 0, j)),
        out_shape=jax.ShapeDtypeStruct((depth, rows, n6), F32),
        compiler_params=_params("parallel", "parallel"),
        name="ada",
    )(cond, w_ada, b_ada.reshape(depth, 1, n6))


def _modulated_norm(x_ref, mod_ref, g_ref, sh_i, sc_i):
    x = x_ref[...]
    y = x * lax.rsqrt(jnp.mean(x * x, axis=-1, keepdims=True) + EPS) * g_ref[...]
    return y * (1.0 + mod_ref[sc_i:sc_i + 1, :]) + mod_ref[sh_i:sh_i + 1, :]


def _prenorm_kernel(x_ref, mod_ref, g_ref, o_ref, *, sh_i, sc_i, pack):
    h = _modulated_norm(x_ref, mod_ref, g_ref, sh_i, sc_i)
    o_ref[...] = _pack_pair(h) if pack else h.astype(BF16)


def _route_kernel(x_ref, mod_ref, g_ref, wr_ref, info_ref, cnt_ref, carry, *, sh_i, sc_i, n_exp):
    i = pl.program_id(0)

    @pl.when(i == 0)
    def _():
        carry[...] = jnp.zeros_like(carry)

    h = _modulated_norm(x_ref, mod_ref, g_ref, sh_i, sc_i)
    w = wr_ref[...]
    hh = h.astype(BF16)
    hl = (h - hh.astype(F32)).astype(BF16)
    wh = w.astype(BF16)
    wl = (w - wh.astype(F32)).astype(BF16)
    logits = _dot(hh, wh) + _dot(hl, wh) + _dot(hh, wl)
    tm = logits.shape[0]
    lane = lax.broadcasted_iota(jnp.int32, (tm, LANES), 1).astype(F32)
    neg = jnp.float32(-jnp.inf)
    l1 = jnp.where(lane < n_exp, logits, neg)
    m1 = jnp.max(l1, axis=-1, keepdims=True)
    i1 = jnp.min(jnp.where(l1 == m1, lane, float(LANES)), axis=-1, keepdims=True)
    l2 = jnp.where(lane == i1, neg, l1)
    m2 = jnp.max(l2, axis=-1, keepdims=True)
    i2 = jnp.min(jnp.where(l2 == m2, lane, float(LANES)), axis=-1, keepdims=True)
    e = jnp.exp(m2 - m1)
    w1 = 1.0 / (1.0 + e)
    w2 = e / (1.0 + e)
    sel1 = lane == i1
    sel2 = lane == i2
    a = jnp.where(sel1 | sel2, 1.0, 0.0)
    r = lax.broadcasted_iota(jnp.int32, (tm, tm), 0)
    c = lax.broadcasted_iota(jnp.int32, (tm, tm), 1)
    before = jnp.where(c < r, 1.0, 0.0).astype(BF16)
    pos = _dot(before, a.astype(BF16)) + carry[...]
    p1 = jnp.sum(jnp.where(sel1, pos, 0.0), axis=-1, keepdims=True)
    p2 = jnp.sum(jnp.where(sel2, pos, 0.0), axis=-1, keepdims=True)
    total = carry[...] + jnp.sum(a, axis=0, keepdims=True)
    carry[...] = total
    cnt_ref[...] = jnp.broadcast_to(total, cnt_ref.shape)
    info = jnp.where(lane == 0, i1, 0.0)
    info = jnp.where(lane == 1, i2, info)
    info = jnp.where(lane == 2, p1, info)
    info = jnp.where(lane == 3, p2, info)
    info = jnp.where(lane == 4, w1, info)
    info = jnp.where(lane == 5, w2, info)
    info_ref[...] = info


def _mod_spec(d, tm, row_of):
    return pl.BlockSpec((None, 6, d), lambda i: (row_of(i * tm), 0, 0))


def _prenorm(x, mod, g, row_of, sh_i, sc_i, pack):
    t, d = x.shape
    tm = row_of.tile(512)
    out = jax.ShapeDtypeStruct((t, d // 2), jnp.uint32) if pack else jax.ShapeDtypeStruct((t, d), BF16)
    ow = d // 2 if pack else d
    return pl.pallas_call(
        functools.partial(_prenorm_kernel, sh_i=sh_i, sc_i=sc_i, pack=pack),
        grid=(t // tm,),
        in_specs=[pl.BlockSpec((tm, d), lambda i: (i, 0)),
                  _mod_spec(d, tm, row_of),
                  pl.BlockSpec((1, d), lambda i: (0, 0))],
        out_specs=pl.BlockSpec((tm, ow), lambda i: (i, 0)),
        out_shape=out,
        compiler_params=_params("parallel"),
        name="prenorm",
    )(x, mod, g.reshape(1, d))


def _route(x, mod, g, w_router, row_of, sh_i, sc_i):
    t, d = x.shape
    n_exp = w_router.shape[1]
    tm = row_of.tile(512)
    wr = jnp.zeros((d, LANES), F32).at[:, :n_exp].set(w_router)
    return pl.pallas_call(
        functools.partial(_route_kernel, sh_i=sh_i, sc_i=sc_i, n_exp=n_exp),
        grid=(t // tm,),
        in_specs=[pl.BlockSpec((tm, d), lambda i: (i, 0)),
                  _mod_spec(d, tm, row_of),
                  pl.BlockSpec((1, d), lambda i: (0, 0)),
                  pl.BlockSpec((d, LANES), lambda i: (0, 0))],
        out_specs=[pl.BlockSpec((tm, LANES), lambda i: (i, 0)),
                   pl.BlockSpec((8, LANES), lambda i: (0, 0))],
        out_shape=[jax.ShapeDtypeStruct((t, LANES), F32),
                   jax.ShapeDtypeStruct((8, LANES), F32)],
        scratch_shapes=[pltpu.VMEM((1, LANES), F32)],
        compiler_params=_params("arbitrary"),
        name="route",
    )(x, mod, g.reshape(1, d), wr)


IN_PROJ_CHUNKS = 4


def _in_proj_kernel(x0_ref, mod0_ref, xc_ref, modn_ref, g_ref, w_ref, o_ref, h_even, h_odd, *, sh_i, sc_i):
    i = pl.program_id(0)
    jc = jnp.minimum(pl.program_id(1), IN_PROJ_CHUNKS - 1)
    tc = xc_ref.shape[0]

    @pl.when((i == 0) & (pl.program_id(1) == 0))
    def _():
        h_even[...] = _modulated_norm(x0_ref, mod0_ref, g_ref, sh_i, sc_i).astype(BF16)

    def step(h_cur, h_next):
        h_next[pl.ds(pl.multiple_of(jc * tc, tc), tc), :] = (
            _modulated_norm(xc_ref, modn_ref, g_ref, sh_i, sc_i).astype(BF16))
        o_ref[...] = _dot(h_cur[...], w_ref[...]).astype(o_ref.dtype)

    @pl.when(i % 2 == 0)
    def _():
        step(h_even, h_odd)

    @pl.when(i % 2 == 1)
    def _():
        step(h_odd, h_even)


def _in_proj(x, mod, g, w, layer, row_of, sh_i, sc_i):
    t, d = x.shape
    n = w.shape[2]
    tm = row_of.tile(1024)
    tn = _pick(n, 1664, LANES)
    ni = t // tm
    tc = tm // IN_PROJ_CHUNKS
    assert n // tn >= IN_PROJ_CHUNKS

    def nxt(i):
        return jnp.minimum(i + 1, ni - 1)

    return pl.pallas_call(
        functools.partial(_in_proj_kernel, sh_i=sh_i, sc_i=sc_i),
        grid=(ni, n // tn),
        in_specs=[pl.BlockSpec((tm, d), lambda i, j: (0, 0), pipeline_mode=pl.Buffered(1)),
                  pl.BlockSpec((None, 6, d), lambda i, j: (row_of(0), 0, 0)),
                  pl.BlockSpec((tc, d), lambda i, j: (nxt(i) * IN_PROJ_CHUNKS
                                                      + jnp.minimum(j, IN_PROJ_CHUNKS - 1), 0)),
                  pl.BlockSpec((None, 6, d), lambda i, j: (row_of(nxt(i) * tm), 0, 0)),
                  pl.BlockSpec((1, d), lambda i, j: (0, 0)),
                  pl.BlockSpec((None, d, tn), lambda i, j: (layer, 0, j))],
        out_specs=pl.BlockSpec((tm, tn), lambda i, j: (i, j)),
        out_shape=jax.ShapeDtypeStruct((t, n), BF16),
        scratch_shapes=[pltpu.VMEM((tm, d), BF16), pltpu.VMEM((tm, d), BF16)],
        compiler_params=_params("arbitrary", "arbitrary"),
        name="in_proj",
    )(x, mod, x, mod, g.reshape(1, d), w)


def _gla_kernel(*refs, n_rows, seqs, has_s0, has_prev, want_state):
    q_ref, k_ref, v_ref, r_ref, alr_ref, wal_ref, bal_ref, gh_ref = refs[:8]
    pos = 8
    s0_ref = None
    if has_s0:
        s0_ref = refs[pos]
        pos += 1
    if has_prev:
        pos += 1
    og_ref = refs[pos]
    pos += 1
    sfin_ref = None
    if want_state:
        sfin_ref = refs[pos]
        pos += 1
    of_scr, ob_scr, st_scr = refs[pos:pos + 3]
    o_scr = (of_scr, ob_scr)

    dk = q_ref.shape[1]
    n_slab = n_rows // SLAB
    n_chunk = SLAB // GLA_CHUNK
    scale = dk ** -0.5

    ri = lax.broadcasted_iota(jnp.int32, (SLAB, SLAB), 0)
    ci = lax.broadcasted_iota(jnp.int32, (SLAB, SLAB), 1)
    shift = GLA_CHUNK.bit_length() - 1
    same = jnp.right_shift(ri, shift) == jnp.right_shift(ci, shift)
    keep = (same & (ci <= ri), same & (ci >= ri))
    sum_mats = tuple(jnp.where(m, 1.0, 0.0).astype(BF16) for m in keep)

    for s in range(seqs):
        for d in range(2):
            if has_s0:
                st_scr[s, d] = s0_ref[d].T
            else:
                st_scr[s, d] = jnp.zeros(st_scr.shape[2:], F32)

    group = math.gcd(n_slab, GLA_GROUP)
    order = (tuple(range(n_chunk)), tuple(range(n_chunk - 1, -1, -1)))
    chunk_rows = [slice(c * GLA_CHUNK, (c + 1) * GLA_CHUNK) for c in range(n_chunk)]

    def body(i, carry):
        jobs = []
        for s in range(seqs):
            jobs += [(0, pl.multiple_of(s * n_rows + (i * group + g) * SLAB, SLAB), s) for g in range(group)]
            jobs += [(1, pl.multiple_of(s * n_rows + (n_slab - 1 - (i * group + g)) * SLAB, SLAB), s)
                     for g in range(group)]
        cols = [slice(d * dk, (d + 1) * dk) for d, _, _ in jobs]
        rows = [pl.ds(r0, SLAB) for _, r0, _ in jobs]
        z = [_dot(alr_ref[rw, :], wal_ref[:, cl]) + bal_ref[:, cl] for rw, cl in zip(rows, cols)]
        parts = []
        for zz in z:
            la = (jnp.minimum(zz, 0.0) - jnp.log1p(jnp.exp(-jnp.abs(zz)))) * (1.0 / GLA_TAU)
            hi = la.astype(BF16)
            rem = la - hi.astype(F32)
            mid = rem.astype(BF16)
            lo = (rem - mid.astype(F32)).astype(BF16)
            parts.append(jnp.concatenate([hi, mid, lo], axis=1))
        ct = [_dot(sum_mats[d], p) for (d, _, _), p in zip(jobs, parts)]
        qd, ki, ke, dec, v = [], [], [], [], []
        for (d, _, _), rw, c3 in zip(jobs, rows, ct):
            cum = c3[:, :dk] + c3[:, dk:2 * dk] + c3[:, 2 * dk:]
            edge = GLA_CHUNK - 1 if d == 0 else 0
            tot_rows = [cum[c * GLA_CHUNK + edge:c * GLA_CHUNK + edge + 1, :] for c in range(n_chunk)]
            tot = jnp.concatenate([jnp.broadcast_to(r, (GLA_CHUNK, dk)) for r in tot_rows], axis=0)
            q = q_ref[rw, :].astype(F32) * scale
            k = k_ref[rw, :].astype(F32)
            qd.append((q * jnp.exp(cum)).astype(BF16))
            ki.append((k * jnp.exp(-cum)).astype(BF16))
            ke.append((k * jnp.exp(tot - cum)).astype(BF16))
            dec.append([jnp.exp(r) for r in tot_rows])
            v.append(v_ref[rw, :])
        n_jobs = len(jobs)
        scores = [_dot_nt(qd[n], ki[n]) for n in range(n_jobs)]
        kv = [[_dot_tn(v[n][sl], ke[n][sl]) for sl in chunk_rows] for n in range(n_jobs)]
        o = [_dot(jnp.where(keep[jobs[n][0]], scores[n], 0.0).astype(BF16), v[n]) for n in range(n_jobs)]
        st_in = [[None] * n_chunk for _ in range(n_jobs)]
        for s in range(seqs):
            for d in range(2):
                st = st_scr[s, d]
                for n in range(n_jobs):
                    if jobs[n][0] != d or jobs[n][2] != s:
                        continue
                    for c in order[d]:
                        st_in[n][c] = st.astype(BF16)
                        st = dec[n][c] * st + kv[n][c]
                st_scr[s, d] = st
        for n, (d, r0, _) in enumerate(jobs):
            for c in order[d]:
                o_inter = _dot_nt(qd[n][chunk_rows[c]], st_in[n][c])
                o_scr[d][pl.ds(r0 + c * GLA_CHUNK, GLA_CHUNK), :] = o[n][chunk_rows[c]] + o_inter
        return carry

    lax.fori_loop(0, n_slab // group, body, 0)

    def finish(i, carry):
        rows = pl.ds(pl.multiple_of(i * SLAB, SLAB), SLAB)
        o = of_scr[rows, :] + ob_scr[rows, :]
        o = o * lax.rsqrt(jnp.mean(o * o, axis=-1, keepdims=True) + EPS) * gh_ref[...]
        og_ref[rows, :] = (o * _silu(r_ref[rows, :].astype(F32))).astype(og_ref.dtype)
        return carry

    lax.fori_loop(0, seqs * n_slab, finish, 0)
    if want_state:
        for s in range(seqs):
            for d in range(2):
                sfin_ref[s, d] = st_scr[s, d].T


def _gla(u, wal, bal, gh, s0, layer, prev, *, row0, n_seq, n_rows, col, want_state):
    t = u.shape[0]
    heads, dk2 = wal.shape[0], wal.shape[2]
    dk = dk2 // 2
    dv = gh.shape[2]
    has_s0 = s0 is not None
    has_prev = prev is not None
    seqs = 1 if has_s0 else math.gcd(n_seq, GLA_GROUP // math.gcd(n_rows // SLAB, GLA_GROUP))
    br = seqs * n_rows
    assert row0 % br == 0 and n_rows % SLAB == 0
    rb0 = row0 // br

    def at(col_units):
        return lambda b, h: (rb0 + b, col_units + h)

    in_specs = [pl.BlockSpec((br, dk), at(col["q"] // dk)),
                pl.BlockSpec((br, dk), at(col["k"] // dk)),
                pl.BlockSpec((br, dv), at(col["v"] // dv)),
                pl.BlockSpec((br, dv), at(col["r"] // dv)),
                pl.BlockSpec((br, LANES), lambda b, h: (rb0 + b, col["alr"] // LANES)),
                pl.BlockSpec((None, LANES, dk2), lambda b, h: (h, 0, 0)),
                pl.BlockSpec((None, 1, dk2), lambda b, h: (h, 0, 0)),
                pl.BlockSpec((None, 1, dv), lambda b, h: (h, 0, 0))]
    args = [u, u, u, u, u, wal, bal, gh]
    if has_s0:
        in_specs.append(pl.BlockSpec((None, None, 2, None, dk, dv), lambda b, h: (b, layer, 0, h, 0, 0)))
        args.append(s0)
    aliases = {}
    if has_prev:
        aliases = {len(args): 0}
        in_specs.append(pl.BlockSpec(memory_space=pl.ANY))
        args.append(prev)
    out_specs = [pl.BlockSpec((br, dv), lambda b, h: (rb0 + b, h))]
    out_shape = [jax.ShapeDtypeStruct((t, heads * dv), BF16)]
    if want_state:
        out_specs.append(pl.BlockSpec((seqs, 2, None, dk, dv), lambda b, h: (b, 0, h, 0, 0)))
        out_shape.append(jax.ShapeDtypeStruct((n_seq, 2, heads, dk, dv), F32))
    res = pl.pallas_call(
        functools.partial(_gla_kernel, n_rows=n_rows, seqs=seqs, has_s0=has_s0, has_prev=has_prev,
                          want_state=want_state),
        grid=(n_seq // seqs, heads),
        in_specs=in_specs,
        out_specs=out_specs,
        out_shape=out_shape,
        input_output_aliases=aliases,
        scratch_shapes=[pltpu.VMEM((br, dv), F32), pltpu.VMEM((br, dv), F32),
                        pltpu.VMEM((seqs, 2, dv, dk), F32)],
        compiler_params=_params("parallel", "parallel"),
        name="gla",
    )(*args)
    return res if want_state else (res[0], None)


def _chan_dft_kernel(x_ref, tab_ref, xc_ref, xs_ref):
    y = _dot(x_ref[...], tab_ref[...])
    w = xc_ref.shape[1]
    xc_ref[...] = y[:, :w].astype(xc_ref.dtype)
    xs_ref[...] = y[:, w:].astype(xs_ref.dtype)


def _chan_dft(u, tab, col_f, width, row0, n_rows):
    gw = tab.shape[0]
    tm = _pick(math.gcd(row0, n_rows) if row0 else n_rows, 2048)
    c0 = col_f // gw
    i0 = row0 // tm
    spec_o = pl.BlockSpec((tm, gw), lambda i, g: (i, g))
    return pl.pallas_call(
        _chan_dft_kernel,
        grid=(n_rows // tm, width // gw),
        in_specs=[pl.BlockSpec((tm, gw), lambda i, g: (i0 + i, c0 + g)),
                  pl.BlockSpec((gw, 2 * gw), lambda i, g: (0, 0))],
        out_specs=[spec_o, spec_o],
        out_shape=[jax.ShapeDtypeStruct((n_rows, width), BF16)] * 2,
        compiler_params=_params("parallel", "parallel"),
        name="chan_dft",
    )(u, tab)


def _chan_fold_kernel(xm_ref, xa_ref, xt_ref, xh_ref, tab_ref, ap_ref, bm_ref, mid_ref):
    m = pl.program_id(1)
    blk, width = xm_ref.shape
    gw = tab_ref.shape[0]
    r = lax.broadcasted_iota(jnp.int32, (blk, blk), 0)
    c = lax.broadcasted_iota(jnp.int32, (blk, blk), 1)
    flip = jnp.where(r + c == blk, 1.0, 0.0).astype(BF16)
    rev = _dot(flip, xa_ref[...])
    first = jnp.where(m > 0, xt_ref[0:1, :].astype(F32), 0.0)
    rev = jnp.where(lax.broadcasted_iota(jnp.int32, (blk, width), 0) == 0, first, rev)
    xm = xm_ref[...].astype(F32)
    plus = (xm + rev).astype(BF16)
    minus = (xm - rev).astype(BF16)
    xh = xh_ref[...]
    tab = tab_ref[...]
    for g in range(width // gw):
        cs = slice(g * gw, (g + 1) * gw)
        ap_ref[:, cs] = _dot(plus[:, cs], tab[:, :gw]).astype(ap_ref.dtype)
        bm_ref[:, cs] = _dot(minus[:, cs], tab[:, gw:]).astype(bm_ref.dtype)
        mid_ref[:, cs] = _dot(xh[:, cs], tab[:, :gw])


def _chan_fold(u, tab, col_f, width, row0, n_seq, n_rows):
    gw = tab.shape[0]
    blk = SLAB
    sub = 16
    nb = n_rows // blk
    nbh = nb // 2
    assert row0 % blk == 0 and n_rows % (2 * blk) == 0
    base = row0 // blk
    assert col_f % width == 0
    c0 = col_f // width

    def first_of(b, m):
        return base + b * nb + jnp.where(m == 0, 0, nb - m)

    half_rows = n_seq * n_rows // 2
    spec_o = pl.BlockSpec((blk, width), lambda b, m: (b * nbh + m, 0))
    return pl.pallas_call(
        _chan_fold_kernel,
        grid=(n_seq, nbh),
        in_specs=[pl.BlockSpec((blk, width), lambda b, m: (base + b * nb + m, c0)),
                  pl.BlockSpec((blk, width), lambda b, m: (base + b * nb + nb - 1 - m, c0)),
                  pl.BlockSpec((sub, width), lambda b, m: (first_of(b, m) * (blk // sub), c0)),
                  pl.BlockSpec((sub, width), lambda b, m: ((base + b * nb + nbh) * (blk // sub), c0)),
                  pl.BlockSpec((gw, 2 * gw), lambda b, m: (0, 0))],
        out_specs=[spec_o, spec_o, pl.BlockSpec((None, sub, width), lambda b, m: (b, 0, 0))],
        out_shape=[jax.ShapeDtypeStruct((half_rows, width), BF16), jax.ShapeDtypeStruct((half_rows, width), BF16),
                   jax.ShapeDtypeStruct((n_seq, sub, width), F32)],
        compiler_params=_params("parallel", "arbitrary"),
        name="chan_fold",
    )(u, u, u, u, tab)


def _pos_dft_kernel(*refs, fold):
    tc_ref, ts_ref, xc_ref, xs_ref = refs[:4]
    o_ref = refs[-1]
    y = _dot(tc_ref[...], xc_ref[...]) + _dot(ts_ref[...], xs_ref[...])
    if fold:
        sign_ref, mid_ref = refs[4:6]
        y = y + sign_ref[...] * mid_ref[0:1, :]
    o_ref[...] = y.astype(o_ref.dtype)


def _pos_dft(tc, ts, xc, xs, prev, *, row0, n_seq, n_rows, sign=None, mid=None):
    t, width = prev.shape
    k_rows = tc.shape[1]
    fold = sign is not None
    assert row0 % n_rows == 0
    rb0 = row0 // n_rows
    to = _pick(n_rows, 512)
    nt = n_rows // to
    once = pl.Buffered(1)
    in_specs = [pl.BlockSpec((to, k_rows), lambda b, i: (i, 0)),
                pl.BlockSpec((to, k_rows), lambda b, i: (i, 0)),
                pl.BlockSpec((k_rows, width), lambda b, i: (b, 0), pipeline_mode=once),
                pl.BlockSpec((k_rows, width), lambda b, i: (b, 0), pipeline_mode=once)]
    args = [tc, ts, xc, xs]
    if fold:
        in_specs += [pl.BlockSpec((to, 1), lambda b, i: (i, 0)),
                     pl.BlockSpec((None,) + mid.shape[1:], lambda b, i: (b, 0, 0))]
        args += [sign, mid]
    in_specs.append(pl.BlockSpec(memory_space=pl.ANY))
    args.append(prev)
    return pl.pallas_call(
        functools.partial(_pos_dft_kernel, fold=fold),
        grid=(n_seq, nt),
        in_specs=in_specs,
        out_specs=pl.BlockSpec((to, width), lambda b, i: ((rb0 + b) * nt + i, 0)),
        out_shape=jax.ShapeDtypeStruct((t, width), BF16),
        input_output_aliases={len(args) - 1: 0},
        compiler_params=_params("parallel", "parallel"),
        name="pos_dft",
    )(*args)


def _dft_tables(n, scale, cols=None, split=64):
    cols = n if cols is None else cols
    k = jnp.arange(cols, dtype=jnp.int32)

    def direct(j):
        ang = ((j[:, None] * k[None, :]) % n).astype(F32) * (2.0 * math.pi / n)
        return jnp.cos(ang), jnp.sin(ang)

    if n <= 4 * split or n % split:
        c, s = direct(jnp.arange(n, dtype=jnp.int32))
    else:
        ca, sa = direct(jnp.arange(0, n, split, dtype=jnp.int32))
        cb, sb = direct(jnp.arange(split, dtype=jnp.int32))
        c = (ca[:, None, :] * cb[None] - sa[:, None, :] * sb[None]).reshape(n, cols)
        s = (sa[:, None, :] * cb[None] + ca[:, None, :] * sb[None]).reshape(n, cols)
    return c * scale, s * scale


def _mix_out_kernel(og_ref, fr_ref, ga_ref, gb_ref, x_ref, mod_ref, g_ref, wa_ref, wb_ref, wo_ref,
                    o_ref, *, ga_i):
    ba = _dot(og_ref[...], wa_ref[...])
    bb = _dot(fr_ref[...], wb_ref[...])
    merged = _sigmoid(ga_ref[...].astype(F32)) * ba + _sigmoid(gb_ref[...].astype(F32)) * bb
    out = _dot(merged.astype(BF16), wo_ref[...])
    y = out * lax.rsqrt(jnp.mean(out * out, axis=-1, keepdims=True) + EPS) * g_ref[...]
    o_ref[...] = x_ref[...] + mod_ref[ga_i:ga_i + 1, :] * y


def _mix_out(og, fr, u, x, mod, g, wa, wb, wo, layer, row_of, col, ga_i):
    t, d = x.shape
    w = og.shape[1]
    tm = row_of.tile(512)
    once = pl.Buffered(1)
    return pl.pallas_call(
        functools.partial(_mix_out_kernel, ga_i=ga_i),
        grid=(t // tm,),
        in_specs=[pl.BlockSpec((tm, w), lambda i: (i, 0)),
                  pl.BlockSpec((tm, w), lambda i: (i, 0)),
                  pl.BlockSpec((tm, d), lambda i: (i, col["ga"] // d)),
                  pl.BlockSpec((tm, d), lambda i: (i, col["gb"] // d)),
                  pl.BlockSpec((tm, d), lambda i: (i, 0)),
                  _mod_spec(d, tm, row_of),
                  pl.BlockSpec((1, d), lambda i: (0, 0)),
                  pl.BlockSpec((None, w, d), lambda i: (layer, 0, 0), pipeline_mode=once),
                  pl.BlockSpec((None, w, d), lambda i: (layer, 0, 0), pipeline_mode=once),
                  pl.BlockSpec((None, d, d), lambda i: (layer, 0, 0), pipeline_mode=once)],
        out_specs=pl.BlockSpec((tm, d), lambda i: (i, 0)),
        out_shape=jax.ShapeDtypeStruct((t, d), F32),
        compiler_params=_params("parallel"),
        name="mix_out",
    )(og, fr, u, u, x, mod, g.reshape(1, d), wa, wb, wo)


def _ffn_kernel(te_ref, nu_ref, x_ref, wg_ref, wu_ref, wd_ref, o_ref, xb, acc):
    g = pl.program_id(0)
    j = pl.program_id(1)
    nj = pl.num_programs(1)
    half = x_ref.shape[1]
    used = g < nu_ref[0]

    @pl.when(used & (j == 0))
    def _():
        lo, hi = _unpack_pair(x_ref[...])
        xb[:, :half] = lo.astype(BF16)
        xb[:, half:] = hi.astype(BF16)
        acc[...] = jnp.zeros_like(acc)

    @pl.when(used)
    def _():
        x = xb[...]
        a = _silu(_dot(x, wg_ref[...])) * _dot(x, wu_ref[...])
        acc[...] += _dot(a.astype(BF16), wd_ref[...])

    @pl.when(j == nj - 1)
    def _():
        @pl.when(used)
        def _():
            o_ref[...] = _pack_pair(acc[...])

        @pl.when(jnp.logical_not(used))
        def _():
            o_ref[...] = jnp.zeros_like(o_ref)


def _ffn(xu, tile_expert, n_used, wg, wu, wd, layer, tm):
    s, half = xu.shape
    _, _, d, f = wg.shape
    tf = _pick(f, 1024 if tm <= 512 else 512, LANES)
    nj = f // tf
    n_tiles = s // tm

    def jj(g, j, nu):
        return jnp.where(g < nu[0], j, nj - 1)

    grid_spec = pltpu.PrefetchScalarGridSpec(
        num_scalar_prefetch=2,
        grid=(n_tiles, nj),
        in_specs=[pl.BlockSpec((tm, half), lambda g, j, te, nu: (g, 0)),
                  pl.BlockSpec((None, None, d, tf), lambda g, j, te, nu: (layer, te[g], 0, jj(g, j, nu))),
                  pl.BlockSpec((None, None, d, tf), lambda g, j, te, nu: (layer, te[g], 0, jj(g, j, nu))),
                  pl.BlockSpec((None, None, tf, d), lambda g, j, te, nu: (layer, te[g], jj(g, j, nu), 0))],
        out_specs=pl.BlockSpec((tm, half), lambda g, j, te, nu: (g, 0)),
        scratch_shapes=[pltpu.VMEM((tm, d), BF16), pltpu.VMEM((tm, d), F32)])
    return pl.pallas_call(
        _ffn_kernel,
        grid_spec=grid_spec,
        out_shape=jax.ShapeDtypeStruct((s, half), jnp.uint32),
        compiler_params=_params("parallel", "arbitrary"),
        name="ffn",
    )(tile_expert, n_used, xu, wg, wu, wd)


def _post_norm_residual(y, x_ref, mod_ref, g_ref, o_ref, ga_i):
    yn = y * lax.rsqrt(jnp.mean(y * y, axis=-1, keepdims=True) + EPS) * g_ref[...]
    o_ref[...] = x_ref[...] + mod_ref[ga_i:ga_i + 1, :] * yn


def _post_kernel(y_ref, x_ref, mod_ref, g_ref, o_ref, *, ga_i):
    lo, hi = _unpack_pair(y_ref[...])
    _post_norm_residual(jnp.concatenate([lo, hi], axis=1), x_ref, mod_ref, g_ref, o_ref, ga_i)


def _post(yu, x, mod, g, row_of, ga_i, row0=0, n_rows=None):
    t, d = x.shape
    n_rows = t if n_rows is None else n_rows
    tm = row_of.tile(512)
    b0 = row0 // tm
    return pl.pallas_call(
        functools.partial(_post_kernel, ga_i=ga_i),
        grid=(n_rows // tm,),
        in_specs=[pl.BlockSpec((tm, d // 2), lambda i: (b0 + i, 0)),
                  pl.BlockSpec((tm, d), lambda i: (b0 + i, 0)),
                  pl.BlockSpec((None, 6, d), lambda i: (row_of((b0 + i) * tm), 0, 0)),
                  pl.BlockSpec((1, d), lambda i: (0, 0))],
        out_specs=pl.BlockSpec((tm, d), lambda i: (i, 0)),
        out_shape=jax.ShapeDtypeStruct((n_rows, d), F32),
        compiler_params=_params("parallel"),
        name="post",
    )(yu, x, mod, g.reshape(1, d))


def _row_copy(src_ref, src_row, dst_ref, dst_row, sem):
    return pltpu.make_async_copy(src_ref.at[pl.ds(src_row, 1), :], dst_ref.at[pl.ds(dst_row, 1), :], sem)


def _dispatch_kernel(s1_ref, s2_ref, x_ref, mod_ref, g_ref, init_ref, xs_ref, buf, sem, *, tb, sh_i, sc_i):
    del init_ref
    i = pl.program_id(0)
    n = pl.num_programs(0)
    cur = i % 2

    def drain(b):
        whole = pltpu.make_async_copy(buf.at[b], xs_ref.at[pl.ds(0, tb), :], sem.at[b])
        whole.wait()
        whole.wait()

    @pl.when(i >= 2)
    def _():
        drain(cur)

    buf[cur] = _pack_pair(_modulated_norm(x_ref, mod_ref, g_ref, sh_i, sc_i))

    def issue(t, carry):
        _row_copy(buf.at[cur], t, xs_ref, s1_ref[0, 0, t], sem.at[cur]).start()
        _row_copy(buf.at[cur], t, xs_ref, s2_ref[0, 0, t], sem.at[cur]).start()
        return carry

    lax.fori_loop(0, tb, issue, 0, unroll=8)

    @pl.when(i == n - 1)
    def _():
        drain(cur)

        @pl.when(n >= 2)
        def _():
            drain(1 - cur)


def _dispatch(x, mod, g, slot1, slot2, n_slots, row_of, sh_i, sc_i):
    t, d = x.shape
    half = d // 2
    tb = row_of.tile(512)
    nb = t // tb
    smem = functools.partial(pl.BlockSpec, (1, 1, tb), lambda i: (i, 0, 0), memory_space=pltpu.SMEM)
    return pl.pallas_call(
        functools.partial(_dispatch_kernel, tb=tb, sh_i=sh_i, sc_i=sc_i),
        grid=(nb,),
        in_specs=[smem(), smem(),
                  pl.BlockSpec((tb, d), lambda i: (i, 0)),
                  _mod_spec(d, tb, row_of),
                  pl.BlockSpec((1, d), lambda i: (0, 0)),
                  pl.BlockSpec(memory_space=pl.ANY)],
        out_specs=pl.BlockSpec(memory_space=pl.ANY),
        out_shape=jax.ShapeDtypeStruct((n_slots, half), jnp.uint32),
        scratch_shapes=[pltpu.VMEM((2, tb, half), jnp.uint32), pltpu.SemaphoreType.DMA((2,))],
        input_output_aliases={5: 0},
        compiler_params=_params("arbitrary"),
        name="dispatch",
    )(slot1.reshape(nb, 1, tb), slot2.reshape(nb, 1, tb), x, mod, g.reshape(1, d),
      jnp.zeros((n_slots, half), jnp.uint32))


def _combine_kernel(s1_ref, s2_ref, info_ref, x_ref, mod_ref, g_ref, ys_ref, o_ref, b1, b2, sem, *,
                    tb, ga_i):
    def issue(t, carry):
        _row_copy(ys_ref, s1_ref[0, 0, t], b1, t, sem).start()
        _row_copy(ys_ref, s2_ref[0, 0, t], b2, t, sem).start()
        return carry

    lax.fori_loop(0, tb, issue, 0, unroll=8)

    pltpu.make_async_copy(ys_ref.at[pl.ds(0, tb), :], b1, sem).wait()
    pltpu.make_async_copy(ys_ref.at[pl.ds(0, tb), :], b2, sem).wait()
    info = info_ref[...]
    w1 = info[:, 4:5]
    w2 = info[:, 5:6]
    lo1, hi1 = _unpack_pair(b1[...])
    lo2, hi2 = _unpack_pair(b2[...])
    y = jnp.concatenate([w1 * lo1 + w2 * lo2, w1 * hi1 + w2 * hi2], axis=1)
    _post_norm_residual(y, x_ref, mod_ref, g_ref, o_ref, ga_i)


def _combine(ys, slot1, slot2, info, x, mod, g, row_of, ga_i, row0=0, n_rows=None):
    t, d = x.shape
    n_rows = t if n_rows is None else n_rows
    half = d // 2
    tb = row_of.tile(512)
    nb = t // tb
    b0 = row0 // tb
    smem = functools.partial(pl.BlockSpec, (1, 1, tb), lambda i: (b0 + i, 0, 0), memory_space=pltpu.SMEM)
    return pl.pallas_call(
        functools.partial(_combine_kernel, tb=tb, ga_i=ga_i),
        grid=(n_rows // tb,),
        in_specs=[smem(), smem(),
                  pl.BlockSpec((tb, LANES), lambda i: (b0 + i, 0)),
                  pl.BlockSpec((tb, d), lambda i: (b0 + i, 0)),
                  pl.BlockSpec((None, 6, d), lambda i: (row_of((b0 + i) * tb), 0, 0)),
                  pl.BlockSpec((1, d), lambda i: (0, 0)),
                  pl.BlockSpec(memory_space=pl.ANY)],
        out_specs=pl.BlockSpec((tb, d), lambda i: (i, 0)),
        out_shape=jax.ShapeDtypeStruct((n_rows, d), F32),
        scratch_shapes=[pltpu.VMEM((tb, half), jnp.uint32), pltpu.VMEM((tb, half), jnp.uint32),
                        pltpu.SemaphoreType.DMA(())],
        compiler_params=_params("arbitrary"),
        name="combine",
    )(slot1.reshape(nb, 1, tb), slot2.reshape(nb, 1, tb), info, x, mod, g.reshape(1, d), ys)


def _route_plan(info, counts, n_exp, tm, n_tiles):
    cnt = counts[0, :n_exp].astype(jnp.int32)
    tiles = (cnt + tm - 1) // tm
    ends = jnp.cumsum(tiles)
    base = (ends - tiles) * tm
    i1 = info[:, 0].astype(jnp.int32)
    i2 = info[:, 1].astype(jnp.int32)
    slot1 = base[i1] + info[:, 2].astype(jnp.int32)
    slot2 = base[i2] + info[:, 3].astype(jnp.int32)
    tile_ids = jnp.arange(n_tiles, dtype=jnp.int32)
    tile_expert = jnp.minimum(jnp.sum((ends[None, :] <= tile_ids[:, None]).astype(jnp.int32), axis=1), n_exp - 1)
    return slot1, slot2, tile_expert, ends[-1:].astype(jnp.int32)


def kernel(x_prompt, x_sample, state_gla, c, c_ctx, w_ada, b_ada, g_mix_pre, g_mix_post, w_in, w_alpha_up, b_alpha, g_head, w_gla_br, w_fn_br, w_out, g_ffn_pre, g_ffn_post, w_ffn_gate, w_ffn_up, w_ffn_down, w_router, w_exp_gate, w_exp_up, w_exp_down):
    bp, np_, d = x_prompt.shape
    bs, ns_, _ = x_sample.shape
    depth = w_ada.shape[0]
    heads = GLA_HEADS
    qk = w_alpha_up.shape[-1]
    dk = qk // heads
    vw = g_head.shape[-1]
    dv = vw // heads
    fw = w_fn_br.shape[1]
    gw = fw // FN_GROUPS
    n_exp = w_router.shape[-1]
    p_rows = bp * np_
    t = p_rows + bs * ns_
    assert dk == LANES and 2 * GLA_LR <= LANES and p_rows % ns_ == 0 and ns_ % np_ == 0

    row_of = _Rows(p_rows, ns_)

    col = {"q": 0, "k": qk, "v": 2 * qk, "r": 2 * qk + vw, "f": 2 * qk + 2 * vw,
           "ga": 2 * qk + 2 * vw + fw, "gb": 2 * qk + 2 * vw + fw + d, "alr": 2 * qk + 2 * vw + fw + 2 * d}
    src_alr = 2 * qk + 2 * vw
    n_u = col["alr"] + LANES
    assert col["ga"] % d == 0 and col["gb"] % d == 0 and col["f"] % gw == 0

    n_cond = 1 + bs
    cond = jnp.zeros((16, d), F32).at[0].set(c_ctx).at[1:n_cond].set(c)
    mod_all = _ada(cond, w_ada, b_ada)[:, :n_cond].reshape(depth, n_cond, 6, d)

    chan_c, chan_s = _dft_tables(gw, gw ** -0.5)
    chan_tab = jnp.concatenate([chan_c, chan_s], axis=1).astype(BF16)
    pos_tabs = {}
    for n in (np_, ns_):
        kk = n if n % (2 * SLAB) else n // 2
        pc, ps = _dft_tables(n, n ** -0.5, kk)
        sign = jnp.where(jnp.arange(n) % 2 == 0, 1.0, -1.0).astype(F32).reshape(n, 1) * n ** -0.5
        pos_tabs[n] = (pc.astype(BF16), (-ps).astype(BF16), sign)

    def fourier(u, fr, row0, n_seq, n):
        tc, ts, sign = pos_tabs[n]
        if tc.shape[1] == n:
            xc, xs = _chan_dft(u, chan_tab, col["f"], fw, row0, n_seq * n)
            return _pos_dft(tc, ts, xc, xs, fr, row0=row0, n_seq=n_seq, n_rows=n)
        ap, bm, mid = _chan_fold(u, chan_tab, col["f"], fw, row0, n_seq, n)
        return _pos_dft(tc, ts, ap, bm, fr, row0=row0, n_seq=n_seq, n_rows=n, sign=sign, mid=mid)

    x = jnp.concatenate([x_prompt.reshape(p_rows, d), x_sample.reshape(bs * ns_, d)], axis=0)
    w_perm = jnp.concatenate(
        [w_in[:, :, :src_alr], w_in[:, :, src_alr + 2 * GLA_LR:], w_in[:, :, src_alr:src_alr + 2 * GLA_LR],
         jnp.zeros((depth, d, LANES - 2 * GLA_LR), F32)], axis=2).astype(BF16)
    wa_all, wb_all, wo_all = w_gla_br.astype(BF16), w_fn_br.astype(BF16), w_out.astype(BF16)
    dense_w = [w[:, None].astype(BF16) for w in (w_ffn_gate, w_ffn_up, w_ffn_down)]
    exp_w = [w.astype(BF16) for w in (w_exp_gate, w_exp_up, w_exp_down)]
    ctx_states = []
    for l in range(depth):
        mod = mod_all[l]
        u = _in_proj(x, mod, g_mix_pre[l], w_perm, l, row_of, 0, 1)
        wup = w_alpha_up[l].reshape(2, GLA_LR, heads, dk)
        wal = jnp.zeros((heads, LANES, 2 * dk), F32)
        wal = wal.at[:, :GLA_LR, :dk].set(wup[0].transpose(1, 0, 2))
        wal = wal.at[:, GLA_LR:2 * GLA_LR, dk:].set(wup[1].transpose(1, 0, 2)).astype(BF16)
        bal = b_alpha[l].reshape(2, heads, 1, dk).transpose(1, 2, 0, 3).reshape(heads, 1, 2 * dk)
        gh = g_head[l].reshape(heads, 1, dv)
        og, s_fin = _gla(u, wal, bal, gh, None, l, jnp.zeros((t, vw), BF16), row0=0, n_seq=bp, n_rows=np_,
                         col=col, want_state=True)
        og, _ = _gla(u, wal, bal, gh, state_gla, l, og, row0=p_rows, n_seq=bs, n_rows=ns_, col=col,
                     want_state=False)
        ctx_states.append(s_fin)
        fr = fourier(u, jnp.zeros((t, fw), BF16), 0, bp, np_)
        fr = fourier(u, fr, p_rows, bs, ns_)
        x = _mix_out(og, fr, u, x, mod, g_mix_post[l], wa_all, wb_all, wo_all, l, row_of, col, 2)
        j = l // 2
        if l % 2 == 0:
            hu = _prenorm(x, mod, g_ffn_pre[l], row_of, 3, 4, pack=True)
            tm = _pick(t, 1024)
            n_tiles = t // tm
            yu = _ffn(hu, jnp.zeros((n_tiles,), jnp.int32), jnp.full((1,), n_tiles, jnp.int32), *dense_w, j, tm)
            finish = functools.partial(_post, yu, x, mod, g_ffn_post[l], row_of, 5)
        else:
            info, counts = _route(x, mod, g_ffn_pre[l], w_router[j], row_of, 3, 4)
            tm = _pick(t, 512)
            n_tiles = (TOP_K * t) // tm + n_exp
            slot1, slot2, tile_expert, n_used = _route_plan(info, counts, n_exp, tm, n_tiles)
            xs_rows = _dispatch(x, mod, g_ffn_pre[l], slot1, slot2, n_tiles * tm, row_of, 3, 4)
            ys_rows = _ffn(xs_rows, tile_expert, n_used, *exp_w, j, tm)
            finish = functools.partial(_combine, ys_rows, slot1, slot2, info, x, mod, g_ffn_post[l], row_of, 5)
        if l + 1 < depth:
            x = finish()
        else:
            y_prompt = finish(0, p_rows).reshape(bp, np_, d)
            y_sample = finish(p_rows, t - p_rows).reshape(bs, ns_, d)
    new_state = jnp.stack(ctx_states, axis=1).astype(x_prompt.dtype)
    return (y_prompt, y_sample, new_state)
```
